```python
import math
import jax, jax.numpy as jnp
from jax import lax
import numpy as np

D_MODEL = 2048
BATCH = 2
SEQ = 8192
DEPTH = 4

GRID_W = 64
CTX_LEN = 256
Q_BLOCK = 128
ROPE_BASE = 10000.0
EPS = 1e-6
N_EVEN = (DEPTH + 1) // 2
N_ODD = DEPTH // 2

A_HEADS = 8
A_Q_LORA = 768
A_KV_LORA = 512
A_NOPE = 128
A_ROPE = 64
A_V = 128
A_QK = A_NOPE + A_ROPE

B_HEADS = 8
B_DIM = 64
B_V = 2 * B_DIM

AB_SIZES = (A_Q_LORA, A_KV_LORA, A_ROPE, B_HEADS * 2 * B_DIM, B_HEADS * 2 * B_DIM, B_HEADS * B_V)
AB_IN = sum(AB_SIZES)
AB_OUT = A_HEADS * A_V + B_HEADS * B_V

C_HEADS = 16
C_KV_HEADS = 4
C_GROUP = C_HEADS // C_KV_HEADS
C_DIM = 128
C_IN = (C_HEADS + 2 * C_KV_HEADS) * C_DIM
C_OUT = C_HEADS * C_DIM

P_HEADS = 8
N_KEYS = 128
N_EXPERTS = N_KEYS * N_KEYS
P_DQ = 256
P_TOPK = 16
P_CHUNK = 128

kernel_name = "hybrid_mla_diff_gqa_peer_dit"


def rmsnorm(x, g):
    xf = x.astype(jnp.float32)
    y = xf * lax.rsqrt(jnp.mean(xf * xf, axis=-1, keepdims=True) + EPS)
    return (y * g.astype(jnp.float32)).astype(x.dtype)


def modulate(h, shift, scale):
    return h * (1 + scale) + shift


def axial_angles(rows, rot_dim):
    n_freq = rot_dim // 4
    freqs = ROPE_BASE ** (-jnp.arange(n_freq, dtype=jnp.float32) / n_freq)
    row = jnp.broadcast_to(jnp.arange(rows, dtype=jnp.float32)[:, None], (rows, GRID_W)).reshape(-1)
    col = jnp.broadcast_to(jnp.arange(GRID_W, dtype=jnp.float32)[None, :], (rows, GRID_W)).reshape(-1)
    return row[:, None] * freqs, col[:, None] * freqs


def rope_half(x, ang):
    half = x.shape[-1] // 2
    a = ang.reshape((ang.shape[0],) + (1,) * (x.ndim - 3) + (ang.shape[-1],))
    cos = jnp.cos(a).astype(x.dtype)
    sin = jnp.sin(a).astype(x.dtype)
    x1, x2 = x[..., :half], x[..., half:]
    return jnp.concatenate([x1 * cos - x2 * sin, x1 * sin + x2 * cos], axis=-1)


def axial_rope(x, ang):
    row_ang, col_ang = ang
    half = x.shape[-1] // 2
    return jnp.concatenate([rope_half(x[..., :half], row_ang), rope_half(x[..., half:], col_ang)], axis=-1)


def sweep_query_blocks(fn, q):
    b, s = q.shape[:2]
    nb = s // Q_BLOCK
    blocks = jnp.moveaxis(q.reshape((b, nb, Q_BLOCK) + q.shape[2:]), 1, 0)
    out = lax.map(fn, blocks)
    return jnp.moveaxis(out, 0, 1).reshape((b, s) + out.shape[3:])


def attend(q, k, v, scale):
    s = jnp.einsum('bqhgd,bkhd->bhgqk', q, k).astype(jnp.float32) * scale
    p = jax.nn.softmax(s, axis=-1).astype(v.dtype)
    return jnp.einsum('bhgqk,bkhd->bqhgd', p, v)


def diff_attend(q, k, v, lam, scale):
    s = jnp.einsum('bqhmd,bkhmd->bhmqk', q, k).astype(jnp.float32) * scale
    p = jax.nn.softmax(s, axis=-1)
    w = (p[:, :, 0] - lam * p[:, :, 1]).astype(v.dtype)
    return jnp.einsum('bhqk,bkhd->bqhd', w, v)


def mixer_ab(h, hz, w_in, g_cq, w_uq, g_ckv, w_ukv, g_qn_a, g_kn_a, lam_vec, g_qn_b, g_kn_b, g_sub,
             w_out, ang_a, ang_b, lam_init, ctx_out):
    split_at = [int(i) for i in np.cumsum(AB_SIZES)[:-1]]

    def project(u):
        bsz, n = u.shape[:2]
        c_q, c_kv, k_rope, bq, bk, bv = jnp.split(u @ w_in, split_at, axis=-1)
        qa = rmsnorm((rmsnorm(c_q, g_cq) @ w_uq).reshape(bsz, n, A_HEADS, A_QK), g_qn_a)
        kv = (rmsnorm(c_kv, g_ckv) @ w_ukv).reshape(bsz, n, A_HEADS, A_NOPE + A_V)
        k_rope = jnp.broadcast_to(k_rope[:, :, None, :], (bsz, n, A_HEADS, A_ROPE))
        ka = rmsnorm(jnp.concatenate([kv[..., :A_NOPE], k_rope], axis=-1), g_kn_a)
        va = kv[..., A_NOPE:]
        qb = rmsnorm(bq.reshape(bsz, n, B_HEADS, 2, B_DIM), g_qn_b)
        kb = rmsnorm(bk.reshape(bsz, n, B_HEADS, 2, B_DIM), g_kn_b)
        vb = bv.reshape(bsz, n, B_HEADS, B_V)
        return qa, ka, va, qb, kb, vb

    def rope_mla(t):
        return jnp.concatenate([t[..., :A_NOPE], axial_rope(t[..., A_NOPE:], ang_a)], axis=-1)

    qa, ka, va, qb, kb, vb = project(h)
    qaz, kaz, vaz, qbz, kbz, vbz = project(hz)
    qa, ka = rope_mla(qa), rope_mla(ka)
    qb, kb = axial_rope(qb, ang_b), axial_rope(kb, ang_b)
    ka_all = jnp.concatenate([ka, kaz], axis=1)
    va_all = jnp.concatenate([va, vaz], axis=1)
    kb_all = jnp.concatenate([kb, kbz], axis=1)
    vb_all = jnp.concatenate([vb, vbz], axis=1)
    lv = lam_vec.astype(jnp.float32)
    lam = jnp.exp(jnp.sum(lv[0] * lv[1])) - jnp.exp(jnp.sum(lv[2] * lv[3])) + lam_init
    sa = A_QK ** -0.5
    sb = B_DIM ** -0.5

    def merge(oa, ob):
        bsz, n = oa.shape[:2]
        ob = rmsnorm(ob, g_sub) * (1.0 - lam_init)
        return jnp.concatenate([oa.reshape(bsz, n, -1), ob.reshape(bsz, n, -1)], axis=-1) @ w_out

    oa = sweep_query_blocks(lambda q: attend(q[:, :, :, None], ka_all, va_all, sa)[:, :, :, 0], qa)
    ob = sweep_query_blocks(lambda q: diff_attend(q, kb_all, vb_all, lam, sb), qb)
    out_x = merge(oa, ob)
    out_z = None
    if ctx_out:
        out_z = merge(attend(qaz[:, :, :, None], kaz, vaz, sa)[:, :, :, 0],
                      diff_attend(qbz, kbz, vbz, lam, sb))
    return out_x, out_z


def mixer_c(h, hz, w_in, g_qn, g_kn, w_out, ang, ctx_out):
    def project(u):
        bsz, n = u.shape[:2]
        p = u @ w_in
        q = p[..., :C_HEADS * C_DIM].reshape(bsz, n, C_KV_HEADS, C_GROUP, C_DIM)
        k = p[..., C_HEADS * C_DIM:(C_HEADS + C_KV_HEADS) * C_DIM].reshape(bsz, n, C_KV_HEADS, C_DIM)
        v = p[..., (C_HEADS + C_KV_HEADS) * C_DIM:].reshape(bsz, n, C_KV_HEADS, C_DIM)
        return rmsnorm(q, g_qn), rmsnorm(k, g_kn), v

    q, k, v = project(h)
    qz, kz, vz = project(hz)
    q, k = axial_rope(q, ang), axial_rope(k, ang)
    k_all = jnp.concatenate([k, kz], axis=1)
    v_all = jnp.concatenate([v, vz], axis=1)
    scale = C_DIM ** -0.5
    bsz, n = h.shape[:2]
    o = sweep_query_blocks(lambda qb: attend(qb, k_all, v_all, scale), q)
    out_x = o.reshape(bsz, n, C_OUT) @ w_out
    out_z = None
    if ctx_out:
        out_z = attend(qz, kz, vz, scale).reshape(bsz, hz.shape[1], C_OUT) @ w_out
    return out_x, out_z


def peer(h, w_pq, sub_keys, expert_u, expert_v):
    bsz, n, d = h.shape
    tokens = h.reshape(-1, P_CHUNK, d)

    def chunk(hc):
        q = (hc @ w_pq).reshape(P_CHUNK, P_HEADS, 2, P_DQ // 2)
        s = jnp.einsum('chpd,hpnd->chpn', q, sub_keys).astype(jnp.float32)
        s_top, i_top = lax.top_k(s, P_TOPK)
        cand = (s_top[:, :, 0, :, None] + s_top[:, :, 1, None, :]).reshape(P_CHUNK, P_HEADS, P_TOPK * P_TOPK)
        cand_idx = (i_top[:, :, 0, :, None] * N_KEYS + i_top[:, :, 1, None, :]).reshape(P_CHUNK, P_HEADS, P_TOPK * P_TOPK)
        best, pos = lax.top_k(cand, P_TOPK)
        idx = jnp.take_along_axis(cand_idx, pos, axis=-1)
        g = jax.nn.softmax(best, axis=-1)
        u = jnp.take(expert_u, idx, axis=0)
        a = jnp.einsum('cd,chkd->chk', hc, u).astype(jnp.float32)
        w = (g * jax.nn.gelu(a)).astype(h.dtype)
        vsel = jnp.take(expert_v, idx, axis=0)
        return jnp.einsum('chk,chkd->cd', w, vsel)

    return lax.map(chunk, tokens).reshape(bsz, n, d)


def setup_inputs(seed: int = 0) -> dict:
    key = jax.random.key(seed)
    ks = iter(jax.random.split(key, 40))
    f32 = jnp.float32

    def nrm(shape, scale):
        return jax.random.normal(next(ks), shape, f32) * scale

    def gain(shape):
        return 1.0 + 0.02 * jax.random.normal(next(ks), shape, f32)

    D = D_MODEL
    return {
        "x": nrm((BATCH, SEQ, D), 1.0),
        "c": nrm((BATCH, D), 1.0),
        "ctx": nrm((BATCH, CTX_LEN, D), 1.0),
        "c_ctx": nrm((D,), 1.0),
        "w_mod": nrm((DEPTH, D, 6 * D), 0.5 * D ** -0.5),
        "b_mod": nrm((DEPTH, 6 * D), 0.02),
        "g_norm1": gain((DEPTH, D)),
        "g_norm2": gain((DEPTH, D)),
        "w_in_ab": nrm((N_EVEN, D, AB_IN), D ** -0.5),
        "g_cq": gain((N_EVEN, A_Q_LORA)),
        "w_uq": nrm((N_EVEN, A_Q_LORA, A_HEADS * A_QK), A_Q_LORA ** -0.5),
        "g_ckv": gain((N_EVEN, A_KV_LORA)),
        "w_ukv": nrm((N_EVEN, A_KV_LORA, A_HEADS * (A_NOPE + A_V)), A_KV_LORA ** -0.5),
        "g_qn_a": gain((N_EVEN, A_QK)),
        "g_kn_a": gain((N_EVEN, A_QK)),
        "lam_vec": nrm((N_EVEN, 4, B_DIM), 0.1),
        "g_qn_b": gain((N_EVEN, B_DIM)),
        "g_kn_b": gain((N_EVEN, B_DIM)),
        "g_sub_b": gain((N_EVEN, B_V)),
        "w_out_ab": nrm((N_EVEN, AB_OUT, D), AB_OUT ** -0.5),
        "w_in_c": nrm((N_ODD, D, C_IN), D ** -0.5),
        "g_qn_c": gain((N_ODD, C_DIM)),
        "g_kn_c": gain((N_ODD, C_DIM)),
        "w_out_c": nrm((N_ODD, C_OUT, D), C_OUT ** -0.5),
        "w_pq": nrm((DEPTH, D, P_HEADS * P_DQ), D ** -0.5),
        "sub_keys": nrm((DEPTH, P_HEADS, 2, N_KEYS, P_DQ // 2), (P_DQ // 2) ** -0.5),
        "expert_u": nrm((DEPTH, N_EXPERTS, D), D ** -0.5),
        "expert_v": nrm((DEPTH, N_EXPERTS, D), (P_HEADS * P_TOPK) ** -0.5),
    }


def reference(x, c, ctx, c_ctx, w_mod, b_mod, g_norm1, g_norm2, w_in_ab, g_cq, w_uq, g_ckv, w_ukv,
              g_qn_a, g_kn_a, lam_vec, g_qn_b, g_kn_b, g_sub_b, w_out_ab, w_in_c, g_qn_c, g_kn_c,
              w_out_c, w_pq, sub_keys, expert_u, expert_v):
    seq = x.shape[1]
    rows = seq // GRID_W
    ang_a = axial_angles(rows, A_ROPE)
    ang_b = axial_angles(rows, B_DIM)
    ang_c = axial_angles(rows, C_DIM)
    z = ctx
    sc = jax.nn.silu(c)
    sz = jax.nn.silu(c_ctx)
    for layer in range(DEPTH):
        last = layer == DEPTH - 1
        e = layer // 2
        mod_x = (sc @ w_mod[layer] + b_mod[layer])[:, None, :]
        mod_z = sz @ w_mod[layer] + b_mod[layer]
        sh1, sc1, gt1, sh2, sc2, gt2 = jnp.split(mod_x, 6, axis=-1)
        zsh1, zsc1, zgt1, zsh2, zsc2, zgt2 = jnp.split(mod_z, 6, axis=-1)

        h = modulate(rmsnorm(x, g_norm1[layer]), sh1, sc1)
        hz = modulate(rmsnorm(z, g_norm1[layer]), zsh1, zsc1)
        if layer % 2 == 0:
            lam_init = 0.8 - 0.6 * math.exp(-0.3 * layer)
            out_x, out_z = mixer_ab(h, hz, w_in_ab[e], g_cq[e], w_uq[e], g_ckv[e], w_ukv[e],
                                    g_qn_a[e], g_kn_a[e], lam_vec[e], g_qn_b[e], g_kn_b[e],
                                    g_sub_b[e], w_out_ab[e], ang_a, ang_b, lam_init, not last)
        else:
            out_x, out_z = mixer_c(h, hz, w_in_c[e], g_qn_c[e], g_kn_c[e], w_out_c[e], ang_c, not last)
        x = x + gt1 * out_x
        x = x + gt2 * peer(modulate(rmsnorm(x, g_norm2[layer]), sh2, sc2),
                           w_pq[layer], sub_keys[layer], expert_u[layer], expert_v[layer])
        if not last:
            z = z + zgt1 * out_z
            z = z + zgt2 * peer(modulate(rmsnorm(z, g_norm2[layer]), zsh2, zsc2),
                                w_pq[layer], sub_keys[layer], expert_u[layer], expert_v[layer])
    return x
```

```python
import functools
import math

import jax
import jax.numpy as jnp
from jax import lax
from jax.experimental import pallas as pl
from jax.experimental.pallas import tpu as pltpu

F32 = jnp.float32
BF16 = jnp.bfloat16

LANE_V7X = 128
VMEM_LIMIT_V7X = 56 * 1024 * 1024

GRID_W = 64
ROPE_BASE = 10000.0
EPS = 1e-6
A_HEADS, A_Q_LORA, A_KV_LORA, A_NOPE, A_ROPE, A_V = 8, 768, 512, 128, 64, 128
A_QK = A_NOPE + A_ROPE
B_HEADS, B_DIM, B_V = 8, 64, 128
C_HEADS, C_KV_HEADS, C_DIM = 16, 4, 128
C_GROUP = C_HEADS // C_KV_HEADS
P_HEADS, N_KEYS, P_DQ, P_TOPK = 8, 128, 256, 16
N_EXPERTS = N_KEYS * N_KEYS
NEG_INF = float("-inf")

AB_CQ, AB_KR, AB_BQ, AB_BK, AB_BV, AB_CKV, AB_COLS = 0, 768, 1024, 2048, 3072, 4096, 4608


def _pick(n, prefs):
    for p in prefs:
        if n % p == 0:
            return p
    return n


def _params(*sem):
    return pltpu.CompilerParams(dimension_semantics=sem, vmem_limit_bytes=VMEM_LIMIT_V7X)


def _mod_kernel(c_ref, w_ref, b_ref, o_ref):
    c = c_ref[...]
    s = c * (1.0 / (1.0 + jnp.exp(-c)))
    o_ref[0] = jnp.dot(s.astype(BF16), w_ref[0].astype(BF16), preferred_element_type=F32) + b_ref[0]


def _modulation(cz, w_mod, b_mod):
    depth, d, n = w_mod.shape
    tn = _pick(n, (1024, 768, 512, 256, 128))
    return pl.pallas_call(
        _mod_kernel,
        grid=(depth, n // tn),
        in_specs=[
            pl.BlockSpec((8, d), lambda l, j: (0, 0)),
            pl.BlockSpec((1, d, tn), lambda l, j: (l, 0, j)),
            pl.BlockSpec((1, 1, tn), lambda l, j: (l, 0, j)),
        ],
        out_specs=pl.BlockSpec((1, 8, tn), lambda l, j: (l, 0, j)),
        out_shape=jax.ShapeDtypeStruct((depth, 8, n), F32),
        compiler_params=_params("parallel", "parallel"),
        name="modulation",
    )(cz, w_mod, b_mod.reshape(depth, 1, n))


def _nmm_kernel(*refs, has_mod, emit_h):
    if has_mod:
        x_ref, g_ref, sh_ref, sc_ref, w_ref = refs[:5]
        rest = refs[5:]
    else:
        x_ref, g_ref, w_ref = refs[:3]
        rest = refs[3:]
    if emit_h:
        o_ref, ho_ref, h_scr = rest
    else:
        o_ref, h_scr = rest

    @pl.when(pl.program_id(1) == 0)
    def _():
        x = x_ref[...].astype(F32)
        y = x * lax.rsqrt(jnp.mean(x * x, axis=-1, keepdims=True) + EPS) * g_ref[...]
        if has_mod:
            y = y * (1.0 + sc_ref[0]) + sh_ref[0]
        hb = y.astype(BF16)
        h_scr[...] = hb
        if emit_h:
            ho_ref[...] = hb

    o_ref[...] = jnp.dot(h_scr[...], w_ref[...], preferred_element_type=F32).astype(o_ref.dtype)


def _nmm(x, g, w, *, k_block=0, shift=None, scale=None, emit_h=False, out_dtype=F32, name):
    n = x.shape[0]
    k, m = w.shape
    has_mod = shift is not None
    n_mod = shift.shape[0] if has_mod else 1
    per_mod = n // n_mod
    tm = _pick(per_mod, (512, 256, 128))
    tn = _pick(m, (512, 768, 256, 128))
    tiles_per_mod = per_mod // tm
    in_specs = [pl.BlockSpec((tm, k), lambda i, j: (i, k_block)), pl.BlockSpec((1, k), lambda i, j: (0, 0))]
    args = [x, g.reshape(1, k)]
    if has_mod:
        mod_spec = pl.BlockSpec((1, 1, k), lambda i, j: (i // tiles_per_mod, 0, 0))
        in_specs += [mod_spec, mod_spec]
        args += [shift, scale]
    in_specs.append(pl.BlockSpec((k, tn), lambda i, j: (0, j)))
    args.append(w)
    out_specs = pl.BlockSpec((tm, tn), lambda i, j: (i, j))
    out_shape = jax.ShapeDtypeStruct((n, m), out_dtype)
    if emit_h:
        out_specs = [out_specs, pl.BlockSpec((tm, k), lambda i, j: (i, 0))]
        out_shape = [out_shape, jax.ShapeDtypeStruct((n, k), BF16)]
    return pl.pallas_call(
        functools.partial(_nmm_kernel, has_mod=has_mod, emit_h=emit_h),
        grid=(n // tm, m // tn),
        in_specs=in_specs,
        out_specs=out_specs,
        out_shape=out_shape,
        scratch_shapes=[pltpu.VMEM((tm, k), BF16)],
        compiler_params=_params("parallel", "arbitrary"),
        name=name,
    )(*args)


def _mm_res_kernel(a_ref, w_ref, res_ref, gate_ref, o_ref):
    acc = jnp.dot(a_ref[...], w_ref[...], preferred_element_type=F32)
    o_ref[...] = res_ref[...] + gate_ref[0] * acc


def _mm_res(a, w, res, gate, *, name):
    n, k = a.shape
    m = w.shape[1]
    per_mod = n // gate.shape[0]
    tm = _pick(per_mod, (512, 256, 128))
    tn = _pick(m, (512, 256, 128))
    tiles_per_mod = per_mod // tm
    return pl.pallas_call(
        _mm_res_kernel,
        grid=(n // tm, m // tn),
        in_specs=[
            pl.BlockSpec((tm, k), lambda i, j: (i, 0)),
            pl.BlockSpec((k, tn), lambda i, j: (0, j)),
            pl.BlockSpec((tm, tn), lambda i, j: (i, j)),
            pl.BlockSpec((1, 1, tn), lambda i, j: (i // tiles_per_mod, 0, j)),
        ],
        out_specs=pl.BlockSpec((tm, tn), lambda i, j: (i, j)),
        out_shape=jax.ShapeDtypeStruct((n, m), F32),
        compiler_params=_params("parallel", "parallel"),
        name=name,
    )(a, w, res, gate)


def _rope(y, cos, sin_up, sin_dn, n_freq):
    w = y.shape[-1]
    return y * cos + pltpu.roll(y, w - n_freq, 1) * sin_up + pltpu.roll(y, n_freq, 1) * sin_dn


def _sumsq(x):
    return jnp.sum(x * x, axis=-1, keepdims=True)


def _prep_aq_kernel(*refs, rope, scale):
    if rope:
        x_ref, gn_ref, gr_ref, cos_ref, su_ref, sd_ref, o_ref = refs
    else:
        x_ref, gn_ref, gr_ref, o_ref = refs
    gn, gr = gn_ref[...], gr_ref[...]
    for h in range(A_HEADS):
        xn = x_ref[:, h * 128:(h + 1) * 128]
        xr = x_ref[:, 1024 + h * 128:1024 + (h + 1) * 128]
        r = lax.rsqrt((_sumsq(xn) + _sumsq(xr)) * (1.0 / A_QK) + EPS)
        yn = xn * r * gn
        yr = xr * r * gr
        if rope:
            yr = _rope(yr, cos_ref[...], su_ref[...], sd_ref[...], A_ROPE // 4)
        o_ref[0, h, 0, :, 0:128] = (yn * scale).astype(BF16)
        o_ref[0, h, 0, :, 128:256] = (yr * scale).astype(BF16)


def _prep_akv_kernel(*refs, rope):
    if rope:
        x_ref, kr_ref, gn_ref, gr_ref, cos_ref, su_ref, sd_ref, k_ref, v_ref = refs
    else:
        x_ref, kr_ref, gn_ref, gr_ref, k_ref, v_ref = refs
    gn = gn_ref[...]
    kr = kr_ref[...]
    ss_r = _sumsq(kr)
    krg = kr * gr_ref[...]
    if rope:
        krg = _rope(krg, cos_ref[...], su_ref[...], sd_ref[...], A_ROPE // 4)
    for h in range(A_HEADS):
        xn = x_ref[:, h * 256:h * 256 + 128]
        r = lax.rsqrt((_sumsq(xn) + ss_r) * (1.0 / A_QK) + EPS)
        k_ref[0, h, :, 0:128] = (xn * r * gn).astype(BF16)
        k_ref[0, h, :, 128:256] = (krg * r).astype(BF16)
        v_ref[0, h] = x_ref[:, h * 256 + 128:(h + 1) * 256].astype(BF16)


def _norm64_pairs(x, g):
    x2 = x * x
    lo = lax.broadcasted_iota(jnp.int32, x.shape, 1) < B_DIM
    s_lo = jnp.sum(jnp.where(lo, x2, 0.0), axis=-1, keepdims=True)
    s_hi = jnp.sum(jnp.where(lo, 0.0, x2), axis=-1, keepdims=True)
    r = jnp.where(lo, lax.rsqrt(s_lo * (1.0 / B_DIM) + EPS), lax.rsqrt(s_hi * (1.0 / B_DIM) + EPS))
    return x * r * g


def _prep_b_kernel(*refs, rope, scale):
    if rope:
        q_ref, k_ref, v_ref, gq_ref, gk_ref, cos_ref, su_ref, sd_ref, qo_ref, ko_ref, vo_ref = refs
    else:
        q_ref, k_ref, v_ref, gq_ref, gk_ref, qo_ref, ko_ref, vo_ref = refs
    for src, g_ref, dst, mul in ((q_ref, gq_ref, qo_ref, scale), (k_ref, gk_ref, ko_ref, None)):
        g = g_ref[...]
        for j in range(B_HEADS):
            y = _norm64_pairs(src[:, j * 128:(j + 1) * 128], g)
            if rope:
                y = _rope(y, cos_ref[...], su_ref[...], sd_ref[...], B_DIM // 4)
            if mul is not None:
                y = y * mul
            yb = y.astype(BF16)
            if dst is qo_ref:
                dst[0, 2 * j, 0] = yb[:, 0:B_DIM]
                dst[0, 2 * j + 1, 0] = yb[:, B_DIM:2 * B_DIM]
            else:
                dst[0, 2 * j] = yb[:, 0:B_DIM]
                dst[0, 2 * j + 1] = yb[:, B_DIM:2 * B_DIM]
    for h in range(B_HEADS):
        vo_ref[0, h] = v_ref[:, h * B_V:(h + 1) * B_V].astype(BF16)


def _prep_c_kernel(*refs, rope, scale):
    if rope:
        q_ref, kv_ref, gq_ref, gk_ref, cos_ref, su_ref, sd_ref, qo_ref, ko_ref, vo_ref = refs
    else:
        q_ref, kv_ref, gq_ref, gk_ref, qo_ref, ko_ref, vo_ref = refs

    def norm_rope(x, g):
        y = x * lax.rsqrt(_sumsq(x) * (1.0 / C_DIM) + EPS) * g
        if rope:
            y = _rope(y, cos_ref[...], su_ref[...], sd_ref[...], C_DIM // 4)
        return y

    gq, gk = gq_ref[...], gk_ref[...]
    for h in range(C_HEADS):
        y = norm_rope(q_ref[:, h * C_DIM:(h + 1) * C_DIM], gq) * scale
        qo_ref[0, h // C_GROUP, h % C_GROUP] = y.astype(BF16)
    for h in range(C_KV_HEADS):
        ko_ref[0, h] = norm_rope(kv_ref[:, h * C_DIM:(h + 1) * C_DIM], gk).astype(BF16)
        vo_ref[0, h] = kv_ref[:, (C_KV_HEADS + h) * C_DIM:(C_KV_HEADS + h + 1) * C_DIM].astype(BF16)


def _prep_call(kernel, ins, in_blocks, consts, tables, outs, *, batch, seq, name):
    ts = _pick(seq, (256, 128))
    n_s = seq // ts
    in_specs = [pl.BlockSpec((ts, w), functools.partial(lambda b, i, c: (b * n_s + i, c), c=c)) for w, c in in_blocks]
    in_specs += [pl.BlockSpec(c.shape, lambda b, i: (0, 0)) for c in consts]
    if tables is not None:
        in_specs += [pl.BlockSpec((ts, LANE_V7X), lambda b, i: (i, 0)) for _ in tables]
    out_specs, out_shape = [], []
    for heads, d in outs:
        nz = len(heads)
        out_specs.append(pl.BlockSpec((1,) + heads + (ts, d),
                                      functools.partial(lambda b, i, nz: (b,) + (0,) * nz + (i, 0), nz=nz)))
        out_shape.append(jax.ShapeDtypeStruct((batch,) + heads + (seq, d), BF16))
    args = list(ins) + list(consts) + (list(tables) if tables is not None else [])
    return pl.pallas_call(
        kernel,
        grid=(batch, n_s),
        in_specs=in_specs,
        out_specs=out_specs,
        out_shape=out_shape,
        compiler_params=_params("parallel", "parallel"),
        name=name,
    )(*args)


def _flash_kernel(q_ref, k_ref, v_ref, o_ref, m_scr, l_scr, acc_scr, *, group, tq, dv):
    kv = pl.program_id(3)

    @pl.when(kv == 0)
    def _():
        m_scr[...] = jnp.full(m_scr.shape, NEG_INF, F32)
        l_scr[...] = jnp.zeros(l_scr.shape, F32)
        acc_scr[...] = jnp.zeros(acc_scr.shape, F32)

    q = q_ref[0, 0].reshape(group * tq, q_ref.shape[-1])
    s = lax.dot_general(q, k_ref[0, 0], (((1,), (1,)), ((), ())), preferred_element_type=F32)
    m_prev = m_scr[...]
    m_new = jnp.maximum(m_prev, jnp.max(s, axis=-1, keepdims=True))
    alpha = jnp.exp(m_prev - m_new)
    p = jnp.exp(s - m_new)
    l_scr[...] = alpha * l_scr[...] + jnp.sum(p, axis=-1, keepdims=True)
    acc_scr[...] = alpha * acc_scr[...] + jnp.dot(p.astype(BF16), v_ref[0, 0], preferred_element_type=F32)
    m_scr[...] = m_new

    @pl.when(kv == pl.num_programs(3) - 1)
    def _():
        out = acc_scr[...] / l_scr[...]
        for g in range(group):
            o_ref[0, :, g * dv:(g + 1) * dv] = out[g * tq:(g + 1) * tq].astype(o_ref.dtype)


def _flash(q, k, v, *, out_dtype, name):
    b, hk, group, s, d = q.shape
    t = k.shape[2]
    hv, dv = v.shape[1], v.shape[3]
    k_per_v = hk // hv
    tq = _pick(s, (512 // group if group > 1 else 512, 256, 128))
    tk = _pick(t, (768, 512, 256, 128))
    rows = group * tq
    return pl.pallas_call(
        functools.partial(_flash_kernel, group=group, tq=tq, dv=dv),
        grid=(b, hk, s // tq, t // tk),
        in_specs=[
            pl.BlockSpec((1, 1, group, tq, d), lambda bi, h, i, j: (bi, h, 0, i, 0)),
            pl.BlockSpec((1, 1, tk, d), lambda bi, h, i, j: (bi, h, j, 0)),
            pl.BlockSpec((1, 1, tk, dv), lambda bi, h, i, j: (bi, h // k_per_v, j, 0)),
        ],
        out_specs=pl.BlockSpec((1, tq, group * dv), lambda bi, h, i, j: (bi, i, h)),
        out_shape=jax.ShapeDtypeStruct((b, s, hk * group * dv), out_dtype),
        scratch_shapes=[pltpu.VMEM((rows, 1), F32), pltpu.VMEM((rows, 1), F32), pltpu.VMEM((rows, dv), F32)],
        compiler_params=_params("parallel", "parallel", "parallel", "arbitrary"),
        name=name,
    )(q, k, v)


def _merge_ab_kernel(oa_ref, ob_ref, g_ref, lv_ref, o_ref, *, lam_init):
    lv = lv_ref[...]
    lam = (jnp.exp(jnp.sum(lv[0:1] * lv[1:2], axis=-1, keepdims=True))
           - jnp.exp(jnp.sum(lv[2:3] * lv[3:4], axis=-1, keepdims=True)) + lam_init)
    n_a = A_HEADS * A_V
    o_ref[:, 0:n_a] = oa_ref[...].astype(BF16)
    g = g_ref[...]
    for h in range(B_HEADS):
        o = ob_ref[:, 2 * h * B_V:(2 * h + 1) * B_V] - lam * ob_ref[:, (2 * h + 1) * B_V:(2 * h + 2) * B_V]
        y = o * lax.rsqrt(_sumsq(o) * (1.0 / B_V) + EPS) * g * (1.0 - lam_init)
        o_ref[:, n_a + h * B_V:n_a + (h + 1) * B_V] = y.astype(BF16)


def _merge_ab(oa, ob, g_sub, lam_vec, lam_init):
    n = oa.shape[0]
    tm = _pick(n, (512, 256, 128))
    wa, wb = oa.shape[1], ob.shape[1]
    wo = wa + B_HEADS * B_V
    return pl.pallas_call(
        functools.partial(_merge_ab_kernel, lam_init=lam_init),
        grid=(n // tm,),
        in_specs=[
            pl.BlockSpec((tm, wa), lambda i: (i, 0)),
            pl.BlockSpec((tm, wb), lambda i: (i, 0)),
            pl.BlockSpec((1, B_V), lambda i: (0, 0)),
            pl.BlockSpec((4, B_DIM), lambda i: (0, 0)),
        ],
        out_specs=pl.BlockSpec((tm, wo), lambda i: (i, 0)),
        out_shape=jax.ShapeDtypeStruct((n, wo), BF16),
        compiler_params=_params("parallel"),
        name="merge_ab",
    )(oa, ob, g_sub.reshape(1, B_V), lam_vec)


def _top_rows(s, k):
    vals = []
    cur = s
    for _ in range(k):
        mx = jnp.max(cur, axis=0, keepdims=True)
        vals.append(mx)
        cur = jnp.where(cur == mx, NEG_INF, cur)
    return jnp.concatenate(vals, axis=0)


def _peer_stats_kernel(q_ref, keys_ref, s1_ref, c1_ref, s2_ref, e2_ref, tau_ref):
    dn = (((1,), (1,)), ((), ()))
    s1 = lax.dot_general(keys_ref[0], q_ref[:, 0:N_KEYS].astype(BF16), dn, preferred_element_type=F32)
    s2 = lax.dot_general(keys_ref[1], q_ref[:, N_KEYS:2 * N_KEYS].astype(BF16), dn, preferred_element_type=F32)
    v1 = _top_rows(s1, P_TOPK)
    v2 = _top_rows(s2, P_TOPK)
    cand = jnp.concatenate([v1[a:a + 1] + v2 for a in range(P_TOPK)], axis=0)
    best = _top_rows(cand, P_TOPK)
    z = jnp.sum(jnp.exp(best - best[0:1]), axis=0, keepdims=True)
    s1_ref[0] = s1
    s2_ref[0] = s2
    c1_ref[0] = jnp.exp(s1 - v1[0:1]) / z
    e2_ref[0] = jnp.exp(s2 - v2[0:1])
    tau_ref[0] = best[P_TOPK - 1:P_TOPK]


def _peer_stats(q, keys):
    n = q.shape[0]
    tm = _pick(n, (256, 128))
    big = pl.BlockSpec((1, N_KEYS, tm), lambda i, h: (h, 0, i))
    big_shape = jax.ShapeDtypeStruct((P_HEADS, N_KEYS, n), F32)
    return pl.pallas_call(
        _peer_stats_kernel,
        grid=(n // tm, P_HEADS),
        in_specs=[
            pl.BlockSpec((tm, P_DQ), lambda i, h: (i, h)),
            pl.BlockSpec((2, N_KEYS, P_DQ // 2), lambda i, h: (h, 0, 0)),
        ],
        out_specs=[big, big, big, big, pl.BlockSpec((1, 1, tm), lambda i, h: (h, 0, i))],
        out_shape=[big_shape, big_shape, big_shape, big_shape, jax.ShapeDtypeStruct((P_HEADS, 1, n), F32)],
        compiler_params=_params("parallel", "parallel"),
        name="peer_stats",
    )(q, keys)


def _gelu_tanh(a):
    return 0.5 * a * (1.0 + jnp.tanh(math.sqrt(2.0 / math.pi) * (a + 0.044715 * (a * a * a))))


def _peer_dense_kernel(h_ref, u_ref, vt_ref, s1_ref, c1_ref, s2_ref, e2_ref, tau_ref, res_ref, gate_ref,
                       o_ref, acc_ref, at_ref, wt_ref, *, te, tm, tc):
    j = pl.program_id(1)

    @pl.when(j == 0)
    def _():
        acc_ref[...] = jnp.zeros(acc_ref.shape, F32)

    at_ref[...] = lax.dot_general(u_ref[...], h_ref[...], (((1,), (1,)), ((), ())), preferred_element_type=F32)
    n_r = te // N_KEYS
    for r in range(n_r):
        i1 = j * n_r + r
        rows = slice(r * N_KEYS, (r + 1) * N_KEYS)
        for c in range(tm // tc):
            cols = slice(c * tc, (c + 1) * tc)
            m = jnp.zeros((N_KEYS, tc), F32)
            for h in range(P_HEADS):
                total = s2_ref[h, :, cols] + s1_ref[h, pl.ds(i1, 1), cols]
                gate = e2_ref[h, :, cols] * c1_ref[h, pl.ds(i1, 1), cols]
                m = m + jnp.where(total >= tau_ref[h, :, cols], gate, 0.0)
            wt_ref[rows, cols] = (m * _gelu_tanh(at_ref[rows, cols])).astype(BF16)
    acc_ref[...] += jnp.dot(vt_ref[...], wt_ref[...], preferred_element_type=F32)

    @pl.when(j == pl.num_programs(1) - 1)
    def _():
        o_ref[...] = res_ref[...] + gate_ref[0] * acc_ref[...].T


def _peer_dense(h, u, vt, stats, res, gate):
    n, d = h.shape
    e = u.shape[0]
    s1, c1, s2, e2, tau = stats
    per_mod = n // gate.shape[0]
    tm = _pick(per_mod, (512, 256, 128))
    te = _pick(e, (512, 256, 128))
    tc = _pick(tm, (256, 128))
    tiles_per_mod = per_mod // tm
    big = pl.BlockSpec((P_HEADS, N_KEYS, tm), lambda i, j: (0, 0, i))
    tok = pl.BlockSpec((tm, d), lambda i, j: (i, 0))
    return pl.pallas_call(
        functools.partial(_peer_dense_kernel, te=te, tm=tm, tc=tc),
        grid=(n // tm, e // te),
        in_specs=[
            tok,
            pl.BlockSpec((te, d), lambda i, j: (j, 0)),
            pl.BlockSpec((d, te), lambda i, j: (0, j)),
            big, big, big, big,
            pl.BlockSpec((P_HEADS, 1, tm), lambda i, j: (0, 0, i)),
            tok,
            pl.BlockSpec((1, 1, d), lambda i, j: (i // tiles_per_mod, 0, 0)),
        ],
        out_specs=tok,
        out_shape=jax.ShapeDtypeStruct((n, d), F32),
        scratch_shapes=[pltpu.VMEM((d, tm), F32), pltpu.VMEM((te, tm), F32), pltpu.VMEM((te, tm), BF16)],
        compiler_params=_params("parallel", "arbitrary"),
        name="peer_dense",
    )(h, u, vt, s1, c1, s2, e2, tau, res, gate)


def _rope_tables(seq, rot_dim):
    n_freq = rot_dim // 4
    freqs = ROPE_BASE ** (-jnp.arange(n_freq, dtype=F32) / n_freq)
    pos = jnp.arange(seq)
    row = (pos // GRID_W).astype(F32)[:, None] * freqs
    col = (pos % GRID_W).astype(F32)[:, None] * freqs
    zero = jnp.zeros_like(row)
    cos = jnp.concatenate([jnp.cos(row), jnp.cos(row), jnp.cos(col), jnp.cos(col)], axis=-1)
    s_up = jnp.concatenate([-jnp.sin(row), zero, -jnp.sin(col), zero], axis=-1)
    s_dn = jnp.concatenate([zero, jnp.sin(row), zero, jnp.sin(col)], axis=-1)
    return cos, s_up, s_dn


def _tile_tables(tabs, reps, width):
    out = []
    for idx, t in enumerate(tabs):
        t = jnp.tile(t, (1, reps))
        pad = width - t.shape[1]
        if pad:
            fill = jnp.ones if idx == 0 else jnp.zeros
            t = jnp.concatenate([t, fill((t.shape[0], pad), F32)], axis=-1)
        out.append(t)
    return out


def _pad_lanes(v, width):
    return jnp.pad(v, (0, width - v.shape[0])).reshape(1, width)


def _layout_w_in_ab(w):
    d = w.shape[0]
    cq, ckv, kr, bq, bk, bv = jnp.split(w, [768, 1280, 1344, 2368, 3392], axis=1)
    return jnp.concatenate([cq, kr, jnp.zeros((d, AB_BQ - AB_KR - A_ROPE), w.dtype), bq, bk, bv, ckv],
                           axis=1).astype(BF16)


def _layout_w_uq(w):
    k = w.shape[0]
    w3 = w.reshape(k, A_HEADS, A_QK)
    nope = w3[:, :, :A_NOPE].reshape(k, A_HEADS * A_NOPE)
    rope = jnp.pad(w3[:, :, A_NOPE:], ((0, 0), (0, 0), (0, LANE_V7X - A_ROPE))).reshape(k, A_HEADS * LANE_V7X)
    return jnp.concatenate([nope, rope], axis=1).astype(BF16)


def _mixer_ab(tok, mod, p, tables, *, batch, seq, rope):
    proj = _nmm(tok, p["g_norm1"], p["w_in"], shift=mod[0], scale=mod[1], name="ab_in")
    qa_raw = _nmm(proj, p["g_cq"], p["w_uq"], k_block=AB_CQ // A_Q_LORA, name="ab_uq")
    kva_raw = _nmm(proj, p["g_ckv"], p["w_ukv"], k_block=AB_CKV // A_KV_LORA, name="ab_ukv")
    sa, sb = A_QK ** -0.5, B_DIM ** -0.5
    (qa,) = _prep_call(
        functools.partial(_prep_aq_kernel, rope=rope, scale=sa), [qa_raw], [(qa_raw.shape[1], 0)],
        [p["gqa_n"], p["gqa_r"]], tables["a"] if rope else None, [((A_HEADS, 1), 2 * LANE_V7X)],
        batch=batch, seq=seq, name="prep_aq")
    ka, va = _prep_call(
        functools.partial(_prep_akv_kernel, rope=rope), [kva_raw, proj],
        [(kva_raw.shape[1], 0), (LANE_V7X, AB_KR // LANE_V7X)],
        [p["gka_n"], p["gka_r"]], tables["a"] if rope else None,
        [((A_HEADS,), 2 * LANE_V7X), ((A_HEADS,), A_V)], batch=batch, seq=seq, name="prep_akv")
    wb = B_HEADS * 2 * B_DIM
    qb, kb, vb = _prep_call(
        functools.partial(_prep_b_kernel, rope=rope, scale=sb), [proj, proj, proj],
        [(wb, AB_BQ // wb), (wb, AB_BK // wb), (wb, AB_BV // wb)],
        [p["gqb"], p["gkb"]], tables["b"] if rope else None,
        [((2 * B_HEADS, 1), B_DIM), ((2 * B_HEADS,), B_DIM), ((B_HEADS,), B_V)],
        batch=batch, seq=seq, name="prep_b")
    return (qa, ka, va), (qb, kb, vb)


def _mixer_c(tok, mod, p, tables, *, batch, seq, rope):
    proj = _nmm(tok, p["g_norm1"], p["w_in"], shift=mod[0], scale=mod[1], name="c_in")
    wq = C_HEADS * C_DIM
    wkv = 2 * C_KV_HEADS * C_DIM
    return _prep_call(
        functools.partial(_prep_c_kernel, rope=rope, scale=C_DIM ** -0.5), [proj, proj],
        [(wq, 0), (wkv, wq // wkv)], [p["gq"], p["gk"]], tables["c"] if rope else None,
        [((C_KV_HEADS, C_GROUP), C_DIM), ((C_KV_HEADS,), C_DIM), ((C_KV_HEADS,), C_DIM)],
        batch=batch, seq=seq, name="prep_c")


def _cat_kv(a, b):
    return jnp.concatenate([a, b], axis=2)


def _peer(tok, mod, gate, p, name):
    q, h = _nmm(tok, p["g_norm2"], p["w_pq"], shift=mod[0], scale=mod[1], emit_h=True, name=name)
    stats = _peer_stats(q, p["keys"])
    return _peer_dense(h, p["u"], p["vt"], stats, tok, gate)


def kernel(x, c, ctx, c_ctx, w_mod, b_mod, g_norm1, g_norm2, w_in_ab, g_cq, w_uq, g_ckv, w_ukv, g_qn_a, g_kn_a,
           lam_vec, g_qn_b, g_kn_b, g_sub_b, w_out_ab, w_in_c, g_qn_c, g_kn_c, w_out_c, w_pq, sub_keys,
           expert_u, expert_v):
    batch, seq, d = x.shape
    ctx_len = ctx.shape[1]
    depth = w_mod.shape[0]
    assert batch + 1 <= 8

    cz = jnp.concatenate([c, c_ctx[None, :], jnp.zeros((8 - batch - 1, d), F32)], axis=0)
    mod_all = _modulation(cz, w_mod, b_mod).reshape(depth, 8, 6, d)

    tab64 = _rope_tables(seq, A_ROPE)
    tables = {
        "a": _tile_tables(tab64, 1, LANE_V7X),
        "b": _tile_tables(tab64, 2, LANE_V7X),
        "c": _rope_tables(seq, C_DIM),
    }

    xt = x.reshape(batch * seq, d)
    zt = ctx.reshape(batch * ctx_len, d)

    for layer in range(depth):
        last = layer == depth - 1
        e = layer // 2
        mx = [mod_all[layer, :batch, i][:, None, :] for i in range(6)]
        mz = [mod_all[layer, batch:batch + 1, i][:, None, :] for i in range(6)]
        pp = {
            "g_norm2": g_norm2[layer], "w_pq": w_pq[layer].astype(BF16),
            "keys": sub_keys[layer].reshape(2 * P_HEADS, N_KEYS, P_DQ // 2).astype(BF16),
            "u": expert_u[layer].astype(BF16), "vt": expert_v[layer].T.astype(BF16),
        }
        if layer % 2 == 0:
            lam_init = 0.8 - 0.6 * math.exp(-0.3 * layer)
            p = {
                "g_norm1": g_norm1[layer], "w_in": _layout_w_in_ab(w_in_ab[e]),
                "g_cq": g_cq[e], "w_uq": _layout_w_uq(w_uq[e]),
                "g_ckv": g_ckv[e], "w_ukv": w_ukv[e].astype(BF16),
                "gqa_n": g_qn_a[e][:A_NOPE].reshape(1, A_NOPE), "gqa_r": _pad_lanes(g_qn_a[e][A_NOPE:], LANE_V7X),
                "gka_n": g_kn_a[e][:A_NOPE].reshape(1, A_NOPE), "gka_r": _pad_lanes(g_kn_a[e][A_NOPE:], LANE_V7X),
                "gqb": jnp.tile(g_qn_b[e], 2).reshape(1, 2 * B_DIM), "gkb": jnp.tile(g_kn_b[e], 2).reshape(1, 2 * B_DIM),
            }
            w_out = w_out_ab[e].astype(BF16)
            (qa, ka, va), (qb, kb, vb) = _mixer_ab(xt, mx, p, tables, batch=batch, seq=seq, rope=True)
            (qaz, kaz, vaz), (qbz, kbz, vbz) = _mixer_ab(zt, mz, p, tables, batch=batch, seq=ctx_len, rope=False)
            oa = _flash(qa, _cat_kv(ka, kaz), _cat_kv(va, vaz), out_dtype=BF16, name="flash_a")
            ob = _flash(qb, _cat_kv(kb, kbz), _cat_kv(vb, vbz), out_dtype=F32, name="flash_b")
            att = _merge_ab(oa.reshape(batch * seq, -1), ob.reshape(batch * seq, -1), g_sub_b[e], lam_vec[e], lam_init)
            if not last:
                oaz = _flash(qaz, kaz, vaz, out_dtype=BF16, name="flash_az")
                obz = _flash(qbz, kbz, vbz, out_dtype=F32, name="flash_bz")
                attz = _merge_ab(oaz.reshape(batch * ctx_len, -1), obz.reshape(batch * ctx_len, -1),
                                 g_sub_b[e], lam_vec[e], lam_init)
        else:
            p = {
                "g_norm1": g_norm1[layer], "w_in": w_in_c[e].astype(BF16),
                "gq": g_qn_c[e].reshape(1, C_DIM), "gk": g_kn_c[e].reshape(1, C_DIM),
            }
            w_out = w_out_c[e].astype(BF16)
            q, k, v = _mixer_c(xt, mx, p, tables, batch=batch, seq=seq, rope=True)
            qz, kz, vz = _mixer_c(zt, mz, p, tables, batch=batch, seq=ctx_len, rope=False)
            att = _flash(q, _cat_kv(k, kz), _cat_kv(v, vz), out_dtype=BF16, name="flash_c").reshape(batch * seq, -1)
            if not last:
                attz = _flash(qz, kz, vz, out_dtype=BF16, name="flash_cz").reshape(batch * ctx_len, -1)

        xt = _mm_res(att, w_out, xt, mx[2], name="out_proj")
        xt = _peer(xt, (mx[3], mx[4]), mx[5], pp, "peer_q")
        if not last:
            zt = _mm_res(attz, w_out, zt, mz[2], name="out_proj_z")
            zt = _peer(zt, (mz[3], mz[4]), mz[5], pp, "peer_qz")
    return xt.reshape(batch, seq, d)
```

```python
import functools
import math

import jax
import jax.numpy as jnp
from jax import lax
from jax.experimental import pallas as pl
from jax.experimental.pallas import tpu as pltpu

F32 = jnp.float32
BF16 = jnp.bfloat16

LANE_V7X = 128
VMEM_LIMIT_V7X = 56 * 1024 * 1024

GRID_W = 64
ROPE_BASE = 10000.0
EPS = 1e-6
A_HEADS, A_Q_LORA, A_KV_LORA, A_NOPE, A_ROPE, A_V = 8, 768, 512, 128, 64, 128
A_QK = A_NOPE + A_ROPE
B_HEADS, B_DIM, B_V = 8, 64, 128
C_HEADS, C_KV_HEADS, C_DIM = 16, 4, 128
C_GROUP = C_HEADS // C_KV_HEADS
P_HEADS, N_KEYS, P_DQ, P_TOPK = 8, 128, 256, 16
N_EXPERTS = N_KEYS * N_KEYS
NEG_INF = float("-inf")
LOG2_E = math.log2(math.e)

AB_CQ, AB_KR, AB_BQ, AB_BK, AB_BV, AB_CKV, AB_COLS = 0, 768, 1024, 2048, 3072, 4096, 4608


def _pick(n, prefs):
    for p in prefs:
        if n % p == 0:
            return p
    return n


def _params(*sem):
    return pltpu.CompilerParams(dimension_semantics=sem, vmem_limit_bytes=VMEM_LIMIT_V7X)


def _mod_kernel(c_ref, w_ref, b_ref, o_ref):
    c = c_ref[...]
    s = c * (1.0 / (1.0 + jnp.exp(-c)))
    o_ref[0] = jnp.dot(s.astype(BF16), w_ref[0].astype(BF16), preferred_element_type=F32) + b_ref[0]


def _modulation(cz, w_mod, b_mod):
    depth, d, n = w_mod.shape
    tn = _pick(n, (1024, 768, 512, 256, 128))
    return pl.pallas_call(
        _mod_kernel,
        grid=(depth, n // tn),
        in_specs=[
            pl.BlockSpec((8, d), lambda l, j: (0, 0)),
            pl.BlockSpec((1, d, tn), lambda l, j: (l, 0, j)),
            pl.BlockSpec((1, 1, tn), lambda l, j: (l, 0, j)),
        ],
        out_specs=pl.BlockSpec((1, 8, tn), lambda l, j: (l, 0, j)),
        out_shape=jax.ShapeDtypeStruct((depth, 8, n), F32),
        compiler_params=_params("parallel", "parallel"),
        name="modulation",
    )(cz, w_mod, b_mod.reshape(depth, 1, n))


def _nmm_kernel(*refs, has_mod, emit_h):
    if has_mod:
        x_ref, g_ref, sh_ref, sc_ref, w_ref = refs[:5]
        rest = refs[5:]
    else:
        x_ref, g_ref, w_ref = refs[:3]
        rest = refs[3:]
    if emit_h:
        o_ref, ho_ref, h_scr = rest
    else:
        o_ref, h_scr = rest

    @pl.when(pl.program_id(1) == 0)
    def _():
        x = x_ref[...].astype(F32)
        y = x * lax.rsqrt(jnp.mean(x * x, axis=-1, keepdims=True) + EPS) * g_ref[...]
        if has_mod:
            y = y * (1.0 + sc_ref[0]) + sh_ref[0]
        hb = y.astype(BF16)
        h_scr[...] = hb
        if emit_h:
            ho_ref[...] = hb

    o_ref[...] = jnp.dot(h_scr[...], w_ref[...], preferred_element_type=F32).astype(o_ref.dtype)


def _nmm(x, g, w, *, k_block=0, shift=None, scale=None, emit_h=False, out_dtype=F32, name):
    n = x.shape[0]
    k, m = w.shape
    has_mod = shift is not None
    n_mod = shift.shape[0] if has_mod else 1
    per_mod = n // n_mod
    tm = _pick(per_mod, (512, 256, 128))
    tn = _pick(m, (512, 768, 256, 128))
    tiles_per_mod = per_mod // tm
    in_specs = [pl.BlockSpec((tm, k), lambda i, j: (i, k_block)), pl.BlockSpec((1, k), lambda i, j: (0, 0))]
    args = [x, g.reshape(1, k)]
    if has_mod:
        mod_spec = pl.BlockSpec((1, 1, k), lambda i, j: (i // tiles_per_mod, 0, 0))
        in_specs += [mod_spec, mod_spec]
        args += [shift, scale]
    in_specs.append(pl.BlockSpec((k, tn), lambda i, j: (0, j)))
    args.append(w)
    out_specs = pl.BlockSpec((tm, tn), lambda i, j: (i, j))
    out_shape = jax.ShapeDtypeStruct((n, m), out_dtype)
    if emit_h:
        out_specs = [out_specs, pl.BlockSpec((tm, k), lambda i, j: (i, 0))]
        out_shape = [out_shape, jax.ShapeDtypeStruct((n, k), BF16)]
    return pl.pallas_call(
        functools.partial(_nmm_kernel, has_mod=has_mod, emit_h=emit_h),
        grid=(n // tm, m // tn),
        in_specs=in_specs,
        out_specs=out_specs,
        out_shape=out_shape,
        scratch_shapes=[pltpu.VMEM((tm, k), BF16)],
        compiler_params=_params("parallel", "arbitrary"),
        name=name,
    )(*args)


def _mm_res_kernel(a_ref, w_ref, res_ref, gate_ref, o_ref):
    acc = jnp.dot(a_ref[...], w_ref[...], preferred_element_type=F32)
    o_ref[...] = res_ref[...] + gate_ref[0] * acc


def _mm_res(a, w, res, gate, *, name):
    n, k = a.shape
    m = w.shape[1]
    per_mod = n // gate.shape[0]
    tm = _pick(per_mod, (512, 256, 128))
    tn = _pick(m, (512, 256, 128))
    tiles_per_mod = per_mod // tm
    return pl.pallas_call(
        _mm_res_kernel,
        grid=(n // tm, m // tn),
        in_specs=[
            pl.BlockSpec((tm, k), lambda i, j: (i, 0)),
            pl.BlockSpec((k, tn), lambda i, j: (0, j)),
            pl.BlockSpec((tm, tn), lambda i, j: (i, j)),
            pl.BlockSpec((1, 1, tn), lambda i, j: (i // tiles_per_mod, 0, j)),
        ],
        out_specs=pl.BlockSpec((tm, tn), lambda i, j: (i, j)),
        out_shape=jax.ShapeDtypeStruct((n, m), F32),
        compiler_params=_params("parallel", "parallel"),
        name=name,
    )(a, w, res, gate)


def _rope(y, cos, sin_up, sin_dn, n_freq):
    w = y.shape[-1]
    return y * cos + pltpu.roll(y, w - n_freq, 1) * sin_up + pltpu.roll(y, n_freq, 1) * sin_dn


def _sumsq(x):
    return jnp.sum(x * x, axis=-1, keepdims=True)


def _prep_aq_kernel(*refs, rope, scale):
    if rope:
        x_ref, gn_ref, gr_ref, cos_ref, su_ref, sd_ref, o_ref = refs
    else:
        x_ref, gn_ref, gr_ref, o_ref = refs
    gn, gr = gn_ref[...], gr_ref[...]
    for h in range(A_HEADS):
        xn = x_ref[:, h * 128:(h + 1) * 128]
        xr = x_ref[:, 1024 + h * 128:1024 + (h + 1) * 128]
        r = lax.rsqrt((_sumsq(xn) + _sumsq(xr)) * (1.0 / A_QK) + EPS)
        yn = xn * r * gn
        yr = xr * r * gr
        if rope:
            yr = _rope(yr, cos_ref[...], su_ref[...], sd_ref[...], A_ROPE // 4)
        o_ref[0, h, 0, :, 0:128] = (yn * scale).astype(BF16)
        o_ref[0, h, 0, :, 128:256] = (yr * scale).astype(BF16)


def _prep_akv_kernel(*refs, rope):
    if rope:
        x_ref, kr_ref, gn_ref, gr_ref, cos_ref, su_ref, sd_ref, k_ref, v_ref = refs
    else:
        x_ref, kr_ref, gn_ref, gr_ref, k_ref, v_ref = refs
    gn = gn_ref[...]
    kr = kr_ref[...]
    ss_r = _sumsq(kr)
    krg = kr * gr_ref[...]
    if rope:
        krg = _rope(krg, cos_ref[...], su_ref[...], sd_ref[...], A_ROPE // 4)
    for h in range(A_HEADS):
        xn = x_ref[:, h * 256:h * 256 + 128]
        r = lax.rsqrt((_sumsq(xn) + ss_r) * (1.0 / A_QK) + EPS)
        k_ref[0, h, :, 0:128] = (xn * r * gn).astype(BF16)
        k_ref[0, h, :, 128:256] = (krg * r).astype(BF16)
        v_ref[0, h] = x_ref[:, h * 256 + 128:(h + 1) * 256].astype(BF16)


def _norm64_pairs(x, g):
    x2 = x * x
    lo = lax.broadcasted_iota(jnp.int32, x.shape, 1) < B_DIM
    s_lo = jnp.sum(jnp.where(lo, x2, 0.0), axis=-1, keepdims=True)
    s_hi = jnp.sum(jnp.where(lo, 0.0, x2), axis=-1, keepdims=True)
    r = jnp.where(lo, lax.rsqrt(s_lo * (1.0 / B_DIM) + EPS), lax.rsqrt(s_hi * (1.0 / B_DIM) + EPS))
    return x * r * g


def _prep_b_kernel(*refs, rope, scale):
    if rope:
        q_ref, k_ref, v_ref, gq_ref, gk_ref, cos_ref, su_ref, sd_ref, qo_ref, ko_ref, vo_ref = refs
    else:
        q_ref, k_ref, v_ref, gq_ref, gk_ref, qo_ref, ko_ref, vo_ref = refs
    for src, g_ref, dst, mul in ((q_ref, gq_ref, qo_ref, scale), (k_ref, gk_ref, ko_ref, None)):
        g = g_ref[...]
        for j in range(B_HEADS):
            y = _norm64_pairs(src[:, j * 128:(j + 1) * 128], g)
            if rope:
                y = _rope(y, cos_ref[...], su_ref[...], sd_ref[...], B_DIM // 4)
            if mul is not None:
                y = y * mul
            yb = y.astype(BF16)
            if dst is qo_ref:
                dst[0, 2 * j, 0] = yb[:, 0:B_DIM]
                dst[0, 2 * j + 1, 0] = yb[:, B_DIM:2 * B_DIM]
            else:
                dst[0, 2 * j] = yb[:, 0:B_DIM]
                dst[0, 2 * j + 1] = yb[:, B_DIM:2 * B_DIM]
    for h in range(B_HEADS):
        vo_ref[0, h] = v_ref[:, h * B_V:(h + 1) * B_V].astype(BF16)


def _prep_c_kernel(*refs, rope, scale):
    if rope:
        q_ref, kv_ref, gq_ref, gk_ref, cos_ref, su_ref, sd_ref, qo_ref, ko_ref, vo_ref = refs
    else:
        q_ref, kv_ref, gq_ref, gk_ref, qo_ref, ko_ref, vo_ref = refs

    def norm_rope(x, g):
        y = x * lax.rsqrt(_sumsq(x) * (1.0 / C_DIM) + EPS) * g
        if rope:
            y = _rope(y, cos_ref[...], su_ref[...], sd_ref[...], C_DIM // 4)
        return y

    gq, gk = gq_ref[...], gk_ref[...]
    for h in range(C_HEADS):
        y = norm_rope(q_ref[:, h * C_DIM:(h + 1) * C_DIM], gq) * scale
        qo_ref[0, h // C_GROUP, h % C_GROUP] = y.astype(BF16)
    for h in range(C_KV_HEADS):
        ko_ref[0, h] = norm_rope(kv_ref[:, h * C_DIM:(h + 1) * C_DIM], gk).astype(BF16)
        vo_ref[0, h] = kv_ref[:, (C_KV_HEADS + h) * C_DIM:(C_KV_HEADS + h + 1) * C_DIM].astype(BF16)


def _prep_call(kernel, ins, in_blocks, consts, tables, outs, *, batch, seq, name):
    ts = _pick(seq, (256, 128))
    n_s = seq // ts
    in_specs = [pl.BlockSpec((ts, w), functools.partial(lambda b, i, c: (b * n_s + i, c), c=c)) for w, c in in_blocks]
    in_specs += [pl.BlockSpec(c.shape, lambda b, i: (0, 0)) for c in consts]
    if tables is not None:
        in_specs += [pl.BlockSpec((ts, LANE_V7X), lambda b, i: (i, 0)) for _ in tables]
    out_specs, out_shape = [], []
    for heads, d in outs:
        nz = len(heads)
        out_specs.append(pl.BlockSpec((1,) + heads + (ts, d),
                                      functools.partial(lambda b, i, nz: (b,) + (0,) * nz + (i, 0), nz=nz)))
        out_shape.append(jax.ShapeDtypeStruct((batch,) + heads + (seq, d), BF16))
    args = list(ins) + list(consts) + (list(tables) if tables is not None else [])
    return pl.pallas_call(
        kernel,
        grid=(batch, n_s),
        in_specs=in_specs,
        out_specs=out_specs,
        out_shape=out_shape,
        compiler_params=_params("parallel", "parallel"),
        name=name,
    )(*args)


def _flash_kernel(q_ref, k_ref, v_ref, o_ref, m_scr, l_scr, acc_scr, *, group, tq, tk, n_sub):
    kv = pl.program_id(3)
    dv = LANE_V7X

    @pl.when(kv == 0)
    def _():
        m_scr[...] = jnp.full(m_scr.shape, NEG_INF, F32)
        l_scr[...] = jnp.zeros(l_scr.shape, F32)
        acc_scr[...] = jnp.zeros(acc_scr.shape, F32)

    q = q_ref[0, 0].reshape(group * tq, q_ref.shape[-1])
    ones = jnp.ones((tk, LANE_V7X), BF16)
    for t in range(n_sub):
        k = k_ref[0, 0, t * tk:(t + 1) * tk, :]
        v1 = jnp.concatenate([v_ref[0, 0, t * tk:(t + 1) * tk, :], ones], axis=1)
        s = lax.dot_general(q, k, (((1,), (1,)), ((), ())), preferred_element_type=F32)
        m_prev = m_scr[...]
        m_new = jnp.maximum(m_prev, jnp.max(s, axis=-1, keepdims=True))
        alpha = jnp.exp2(m_prev - m_new)
        p = jnp.exp2(s - jnp.concatenate([m_new] * (tk // LANE_V7X), axis=1))
        pv = jnp.dot(p.astype(BF16), v1, preferred_element_type=F32)
        acc_scr[...] = alpha * acc_scr[...] + pv[:, :dv]
        l_scr[...] = alpha * l_scr[...] + pv[:, dv:]
        m_scr[...] = m_new

    @pl.when(kv == pl.num_programs(3) - 1)
    def _():
        out = acc_scr[...] / l_scr[...]
        for g in range(group):
            o_ref[0, :, g * dv:(g + 1) * dv] = out[g * tq:(g + 1) * tq].astype(o_ref.dtype)


def _flash(q, k, v, *, out_dtype, name, rows_pref=512, tk_pref=256, sub_pref=33):
    b, hk, group, s, d = q.shape
    t = k.shape[2]
    hv, dv = v.shape[1], v.shape[3]
    assert dv == LANE_V7X
    k_per_v = hk // hv
    tq = _pick(s, (rows_pref // group, 256 // group, 128 // group))
    tk = _pick(t, (tk_pref, 256, 128))
    n_sub = _pick(t // tk, (sub_pref, 11, 8, 4, 3, 2, 1))
    rows = group * tq
    stat = pltpu.VMEM((rows, LANE_V7X), F32)
    return pl.pallas_call(
        functools.partial(_flash_kernel, group=group, tq=tq, tk=tk, n_sub=n_sub),
        grid=(b, hk, s // tq, t // (tk * n_sub)),
        in_specs=[
            pl.BlockSpec((1, 1, group, tq, d), lambda bi, h, i, j: (bi, h, 0, i, 0)),
            pl.BlockSpec((1, 1, tk * n_sub, d), lambda bi, h, i, j: (bi, h, j, 0)),
            pl.BlockSpec((1, 1, tk * n_sub, dv), lambda bi, h, i, j: (bi, h // k_per_v, j, 0)),
        ],
        out_specs=pl.BlockSpec((1, tq, group * dv), lambda bi, h, i, j: (bi, i, h)),
        out_shape=jax.ShapeDtypeStruct((b, s, hk * group * dv), out_dtype),
        scratch_shapes=[stat, stat, stat],
        compiler_params=_params("parallel", "parallel", "parallel", "arbitrary"),
        name=name,
    )(q, k, v)


def _merge_ab_kernel(oa_ref, ob_ref, g_ref, lv_ref, o_ref, *, lam_init):
    lv = lv_ref[...]
    lam = (jnp.exp(jnp.sum(lv[0:1] * lv[1:2], axis=-1, keepdims=True))
           - jnp.exp(jnp.sum(lv[2:3] * lv[3:4], axis=-1, keepdims=True)) + lam_init)
    n_a = A_HEADS * A_V
    o_ref[:, 0:n_a] = oa_ref[...].astype(BF16)
    g = g_ref[...]
    for h in range(B_HEADS):
        o = ob_ref[:, 2 * h * B_V:(2 * h + 1) * B_V] - lam * ob_ref[:, (2 * h + 1) * B_V:(2 * h + 2) * B_V]
        y = o * lax.rsqrt(_sumsq(o) * (1.0 / B_V) + EPS) * g * (1.0 - lam_init)
        o_ref[:, n_a + h * B_V:n_a + (h + 1) * B_V] = y.astype(BF16)


def _merge_ab(oa, ob, g_sub, lam_vec, lam_init):
    n = oa.shape[0]
    tm = _pick(n, (512, 256, 128))
    wa, wb = oa.shape[1], ob.shape[1]
    wo = wa + B_HEADS * B_V
    return pl.pallas_call(
        functools.partial(_merge_ab_kernel, lam_init=lam_init),
        grid=(n // tm,),
        in_specs=[
            pl.BlockSpec((tm, wa), lambda i: (i, 0)),
            pl.BlockSpec((tm, wb), lambda i: (i, 0)),
            pl.BlockSpec((1, B_V), lambda i: (0, 0)),
            pl.BlockSpec((4, B_DIM), lambda i: (0, 0)),
        ],
        out_specs=pl.BlockSpec((tm, wo), lambda i: (i, 0)),
        out_shape=jax.ShapeDtypeStruct((n, wo), BF16),
        compiler_params=_params("parallel"),
        name="merge_ab",
    )(oa, ob, g_sub.reshape(1, B_V), lam_vec)


def _top_rows(s, k):
    vals = []
    cur = s
    for _ in range(k):
        mx = jnp.max(cur, axis=0, keepdims=True)
        vals.append(mx)
        cur = jnp.where(cur == mx, NEG_INF, cur)
    return jnp.concatenate(vals, axis=0)


def _peer_stats_kernel(q_ref, keys_ref, s1_ref, c1_ref, s2_ref, e2_ref, tau_ref):
    dn = (((1,), (1,)), ((), ()))
    s1 = lax.dot_general(keys_ref[0], q_ref[:, 0:N_KEYS].astype(BF16), dn, preferred_element_type=F32)
    s2 = lax.dot_general(keys_ref[1], q_ref[:, N_KEYS:2 * N_KEYS].astype(BF16), dn, preferred_element_type=F32)
    v1 = _top_rows(s1, P_TOPK)
    v2 = _top_rows(s2, P_TOPK)
    cand = jnp.concatenate([v1[a:a + 1] + v2 for a in range(P_TOPK)], axis=0)
    best = _top_rows(cand, P_TOPK)
    z = jnp.sum(jnp.exp(best - best[0:1]), axis=0, keepdims=True)
    s1_ref[0] = s1
    s2_ref[0] = s2
    c1_ref[0] = jnp.exp(s1 - v1[0:1]) / z
    e2_ref[0] = jnp.exp(s2 - v2[0:1])
    tau_ref[0] = best[P_TOPK - 1:P_TOPK]


def _peer_stats(q, keys):
    n = q.shape[0]
    tm = _pick(n, (256, 128))
    big = pl.BlockSpec((1, N_KEYS, tm), lambda i, h: (h, 0, i))
    big_shape = jax.ShapeDtypeStruct((P_HEADS, N_KEYS, n), F32)
    return pl.pallas_call(
        _peer_stats_kernel,
        grid=(n // tm, P_HEADS),
        in_specs=[
            pl.BlockSpec((tm, P_DQ), lambda i, h: (i, h)),
            pl.BlockSpec((2, N_KEYS, P_DQ // 2), lambda i, h: (h, 0, 0)),
        ],
        out_specs=[big, big, big, big, pl.BlockSpec((1, 1, tm), lambda i, h: (h, 0, i))],
        out_shape=[big_shape, big_shape, big_shape, big_shape, jax.ShapeDtypeStruct((P_HEADS, 1, n), F32)],
        compiler_params=_params("parallel", "parallel"),
        name="peer_stats",
    )(q, keys)


def _gelu_tanh(a):
    return 0.5 * a * (1.0 + jnp.tanh(math.sqrt(2.0 / math.pi) * (a + 0.044715 * (a * a * a))))


def _peer_dense_kernel(h_ref, u_ref, vt_ref, s1_ref, c1_ref, s2_ref, e2_ref, tau_ref, res_ref, gate_ref,
                       o_ref, acc_ref, at_ref, wt_ref, *, te, tm, tc):
    j = pl.program_id(1)

    @pl.when(j == 0)
    def _():
        acc_ref[...] = jnp.zeros(acc_ref.shape, F32)

    at_ref[...] = lax.dot_general(u_ref[...], h_ref[...], (((1,), (1,)), ((), ())), preferred_element_type=F32)
    n_r = te // N_KEYS
    for r in range(n_r):
        i1 = j * n_r + r
        rows = slice(r * N_KEYS, (r + 1) * N_KEYS)
        for c in range(tm // tc):
            cols = slice(c * tc, (c + 1) * tc)
            m = jnp.zeros((N_KEYS, tc), F32)
            for h in range(P_HEADS):
                total = s2_ref[h, :, cols] + s1_ref[h, pl.ds(i1, 1), cols]
                gate = e2_ref[h, :, cols] * c1_ref[h, pl.ds(i1, 1), cols]
                m = m + jnp.where(total >= tau_ref[h, :, cols], gate, 0.0)
            wt_ref[rows, cols] = (m * _gelu_tanh(at_ref[rows, cols])).astype(BF16)
    acc_ref[...] += jnp.dot(vt_ref[...], wt_ref[...], preferred_element_type=F32)

    @pl.when(j == pl.num_programs(1) - 1)
    def _():
        o_ref[...] = res_ref[...] + gate_ref[0] * acc_ref[...].T


def _peer_dense(h, u, vt, stats, res, gate):
    n, d = h.shape
    e = u.shape[0]
    s1, c1, s2, e2, tau = stats
    per_mod = n // gate.shape[0]
    tm = _pick(per_mod, (512, 256, 128))
    te = _pick(e, (512, 256, 128))
    tc = _pick(tm, (256, 128))
    tiles_per_mod = per_mod // tm
    big = pl.BlockSpec((P_HEADS, N_KEYS, tm), lambda i, j: (0, 0, i))
    tok = pl.BlockSpec((tm, d), lambda i, j: (i, 0))
    return pl.pallas_call(
        functools.partial(_peer_dense_kernel, te=te, tm=tm, tc=tc),
        grid=(n // tm, e // te),
        in_specs=[
            tok,
            pl.BlockSpec((te, d), lambda i, j: (j, 0)),
            pl.BlockSpec((d, te), lambda i, j: (0, j)),
            big, big, big, big,
            pl.BlockSpec((P_HEADS, 1, tm), lambda i, j: (0, 0, i)),
            tok,
            pl.BlockSpec((1, 1, d), lambda i, j: (i // tiles_per_mod, 0, 0)),
        ],
        out_specs=tok,
        out_shape=jax.ShapeDtypeStruct((n, d), F32),
        scratch_shapes=[pltpu.VMEM((d, tm), F32), pltpu.VMEM((te, tm), F32), pltpu.VMEM((te, tm), BF16)],
        compiler_params=_params("parallel", "arbitrary"),
        name="peer_dense",
    )(h, u, vt, s1, c1, s2, e2, tau, res, gate)


def _rope_tables(seq, rot_dim):
    n_freq = rot_dim // 4
    freqs = ROPE_BASE ** (-jnp.arange(n_freq, dtype=F32) / n_freq)
    pos = jnp.arange(seq)
    row = (pos // GRID_W).astype(F32)[:, None] * freqs
    col = (pos % GRID_W).astype(F32)[:, None] * freqs
    zero = jnp.zeros_like(row)
    cos = jnp.concatenate([jnp.cos(row), jnp.cos(row), jnp.cos(col), jnp.cos(col)], axis=-1)
    s_up = jnp.concatenate([-jnp.sin(row), zero, -jnp.sin(col), zero], axis=-1)
    s_dn = jnp.concatenate([zero, jnp.sin(row), zero, jnp.sin(col)], axis=-1)
    return cos, s_up, s_dn


def _tile_tables(tabs, reps, width):
    out = []
    for idx, t in enumerate(tabs):
        t = jnp.tile(t, (1, reps))
        pad = width - t.shape[1]
        if pad:
            fill = jnp.ones if idx == 0 else jnp.zeros
            t = jnp.concatenate([t, fill((t.shape[0], pad), F32)], axis=-1)
        out.append(t)
    return out


def _pad_lanes(v, width):
    return jnp.pad(v, (0, width - v.shape[0])).reshape(1, width)


def _layout_w_in_ab(w):
    d = w.shape[0]
    cq, ckv, kr, bq, bk, bv = jnp.split(w, [768, 1280, 1344, 2368, 3392], axis=1)
    return jnp.concatenate([cq, kr, jnp.zeros((d, AB_BQ - AB_KR - A_ROPE), w.dtype), bq, bk, bv, ckv],
                           axis=1).astype(BF16)


def _layout_w_uq(w):
    k = w.shape[0]
    w3 = w.reshape(k, A_HEADS, A_QK)
    nope = w3[:, :, :A_NOPE].reshape(k, A_HEADS * A_NOPE)
    rope = jnp.pad(w3[:, :, A_NOPE:], ((0, 0), (0, 0), (0, LANE_V7X - A_ROPE))).reshape(k, A_HEADS * LANE_V7X)
    return jnp.concatenate([nope, rope], axis=1).astype(BF16)


def _mixer_ab(tok, mod, p, tables, *, batch, seq, rope):
    proj = _nmm(tok, p["g_norm1"], p["w_in"], shift=mod[0], scale=mod[1], name="ab_in")
    qa_raw = _nmm(proj, p["g_cq"], p["w_uq"], k_block=AB_CQ // A_Q_LORA, name="ab_uq")
    kva_raw = _nmm(proj, p["g_ckv"], p["w_ukv"], k_block=AB_CKV // A_KV_LORA, name="ab_ukv")
    sa, sb = A_QK ** -0.5 * LOG2_E, B_DIM ** -0.5 * LOG2_E
    (qa,) = _prep_call(
        functools.partial(_prep_aq_kernel, rope=rope, scale=sa), [qa_raw], [(qa_raw.shape[1], 0)],
        [p["gqa_n"], p["gqa_r"]], tables["a"] if rope else None, [((A_HEADS, 1), 2 * LANE_V7X)],
        batch=batch, seq=seq, name="prep_aq")
    ka, va = _prep_call(
        functools.partial(_prep_akv_kernel, rope=rope), [kva_raw, proj],
        [(kva_raw.shape[1], 0), (LANE_V7X, AB_KR // LANE_V7X)],
        [p["gka_n"], p["gka_r"]], tables["a"] if rope else None,
        [((A_HEADS,), 2 * LANE_V7X), ((A_HEADS,), A_V)], batch=batch, seq=seq, name="prep_akv")
    wb = B_HEADS * 2 * B_DIM
    qb, kb, vb = _prep_call(
        functools.partial(_prep_b_kernel, rope=rope, scale=sb), [proj, proj, proj],
        [(wb, AB_BQ // wb), (wb, AB_BK // wb), (wb, AB_BV // wb)],
        [p["gqb"], p["gkb"]], tables["b"] if rope else None,
        [((2 * B_HEADS, 1), B_DIM), ((2 * B_HEADS,), B_DIM), ((B_HEADS,), B_V)],
        batch=batch, seq=seq, name="prep_b")
    return (qa, ka, va), (qb, kb, vb)


def _mixer_c(tok, mod, p, tables, *, batch, seq, rope):
    proj = _nmm(tok, p["g_norm1"], p["w_in"], shift=mod[0], scale=mod[1], name="c_in")
    wq = C_HEADS * C_DIM
    wkv = 2 * C_KV_HEADS * C_DIM
    return _prep_call(
        functools.partial(_prep_c_kernel, rope=rope, scale=C_DIM ** -0.5 * LOG2_E), [proj, proj],
        [(wq, 0), (wkv, wq // wkv)], [p["gq"], p["gk"]], tables["c"] if rope else None,
        [((C_KV_HEADS, C_GROUP), C_DIM), ((C_KV_HEADS,), C_DIM), ((C_KV_HEADS,), C_DIM)],
        batch=batch, seq=seq, name="prep_c")


def _cat_kv(a, b):
    return jnp.concatenate([a, b], axis=2)


def _peer(tok, mod, gate, p, name):
    q, h = _nmm(tok, p["g_norm2"], p["w_pq"], shift=mod[0], scale=mod[1], emit_h=True, name=name)
    stats = _peer_stats(q, p["keys"])
    return _peer_dense(h, p["u"], p["vt"], stats, tok, gate)


def kernel(x, c, ctx, c_ctx, w_mod, b_mod, g_norm1, g_norm2, w_in_ab, g_cq, w_uq, g_ckv, w_ukv, g_qn_a, g_kn_a,
           lam_vec, g_qn_b, g_kn_b, g_sub_b, w_out_ab, w_in_c, g_qn_c, g_kn_c, w_out_c, w_pq, sub_keys,
           expert_u, expert_v):
    batch, seq, d = x.shape
    ctx_len = ctx.shape[1]
    depth = w_mod.shape[0]
    assert batch + 1 <= 8

    cz = jnp.concatenate([c, c_ctx[None, :], jnp.zeros((8 - batch - 1, d), F32)], axis=0)
    mod_all = _modulation(cz, w_mod, b_mod).reshape(depth, 8, 6, d)

    tab64 = _rope_tables(seq, A_ROPE)
    tables = {
        "a": _tile_tables(tab64, 1, LANE_V7X),
        "b": _tile_tables(tab64, 2, LANE_V7X),
        "c": _rope_tables(seq, C_DIM),
    }

    xt = x.reshape(batch * seq, d)
    zt = ctx.reshape(batch * ctx_len, d)

    for layer in range(depth):
        last = layer == depth - 1
        e = layer // 2
        mx = [mod_all[layer, :batch, i][:, None, :] for i in range(6)]
        mz = [mod_all[layer, batch:batch + 1, i][:, None, :] for i in range(6)]
        pp = {
            "g_norm2": g_norm2[layer], "w_pq": w_pq[layer].astype(BF16),
            "keys": sub_keys[layer].reshape(2 * P_HEADS, N_KEYS, P_DQ // 2).astype(BF16),
            "u": expert_u[layer].astype(BF16), "vt": expert_v[layer].T.astype(BF16),
        }
        if layer % 2 == 0:
            lam_init = 0.8 - 0.6 * math.exp(-0.3 * layer)
            p = {
                "g_norm1": g_norm1[layer], "w_in": _layout_w_in_ab(w_in_ab[e]),
                "g_cq": g_cq[e], "w_uq": _layout_w_uq(w_uq[e]),
                "g_ckv": g_ckv[e], "w_ukv": w_ukv[e].astype(BF16),
                "gqa_n": g_qn_a[e][:A_NOPE].reshape(1, A_NOPE), "gqa_r": _pad_lanes(g_qn_a[e][A_NOPE:], LANE_V7X),
                "gka_n": g_kn_a[e][:A_NOPE].reshape(1, A_NOPE), "gka_r": _pad_lanes(g_kn_a[e][A_NOPE:], LANE_V7X),
                "gqb": jnp.tile(g_qn_b[e], 2).reshape(1, 2 * B_DIM), "gkb": jnp.tile(g_kn_b[e], 2).reshape(1, 2 * B_DIM),
            }
            w_out = w_out_ab[e].astype(BF16)
            (qa, ka, va), (qb, kb, vb) = _mixer_ab(xt, mx, p, tables, batch=batch, seq=seq, rope=True)
            (qaz, kaz, vaz), (qbz, kbz, vbz) = _mixer_ab(zt, mz, p, tables, batch=batch, seq=ctx_len, rope=False)
            oa = _flash(qa, _cat_kv(ka, kaz), _cat_kv(va, vaz), out_dtype=BF16, name="flash_a")
            ob = _flash(qb, _cat_kv(kb, kbz), _cat_kv(vb, vbz), out_dtype=F32, name="flash_b")
            att = _merge_ab(oa.reshape(batch * seq, -1), ob.reshape(batch * seq, -1), g_sub_b[e], lam_vec[e], lam_init)
            if not last:
                oaz = _flash(qaz, kaz, vaz, out_dtype=BF16, name="flash_az")
                obz = _flash(qbz, kbz, vbz, out_dtype=F32, name="flash_bz")
                attz = _merge_ab(oaz.reshape(batch * ctx_len, -1), obz.reshape(batch * ctx_len, -1),
                                 g_sub_b[e], lam_vec[e], lam_init)
        else:
            p = {
                "g_norm1": g_norm1[layer], "w_in": w_in_c[e].astype(BF16),
                "gq": g_qn_c[e].reshape(1, C_DIM), "gk": g_kn_c[e].reshape(1, C_DIM),
            }
            w_out = w_out_c[e].astype(BF16)
            q, k, v = _mixer_c(xt, mx, p, tables, batch=batch, seq=seq, rope=True)
            qz, kz, vz = _mixer_c(zt, mz, p, tables, batch=batch, seq=ctx_len, rope=False)
            att = _flash(q, _cat_kv(k, kz), _cat_kv(v, vz), out_dtype=BF16, name="flash_c").reshape(batch * seq, -1)
            if not last:
                attz = _flash(qz, kz, vz, out_dtype=BF16, name="flash_cz").reshape(batch * ctx_len, -1)

        xt = _mm_res(att, w_out, xt, mx[2], name="out_proj")
        xt = _peer(xt, (mx[3], mx[4]), mx[5], pp, "peer_q")
        if not last:
            zt = _mm_res(attz, w_out, zt, mz[2], name="out_proj_z")
            zt = _peer(zt, (mz[3], mz[4]), mz[5], pp, "peer_qz")
    return xt.reshape(batch, seq, d)
```

```python
import functools
import math

import jax
import jax.numpy as jnp
from jax import lax
from jax.experimental import pallas as pl
from jax.experimental.pallas import tpu as pltpu

F32 = jnp.float32
BF16 = jnp.bfloat16

LANE_V7X = 128
VMEM_LIMIT_V7X = 56 * 1024 * 1024

GRID_W = 64
ROPE_BASE = 10000.0
EPS = 1e-6
A_HEADS, A_Q_LORA, A_KV_LORA, A_NOPE, A_ROPE, A_V = 8, 768, 512, 128, 64, 128
A_QK = A_NOPE + A_ROPE
B_HEADS, B_DIM, B_V = 8, 64, 128
C_HEADS, C_KV_HEADS, C_DIM = 16, 4, 128
C_GROUP = C_HEADS // C_KV_HEADS
P_HEADS, N_KEYS, P_DQ, P_TOPK = 8, 128, 256, 16
N_EXPERTS = N_KEYS * N_KEYS
NEG_INF = float("-inf")
LOG2_E = math.log2(math.e)

AB_CQ, AB_KR, AB_BQ, AB_BK, AB_BV, AB_CKV, AB_COLS = 0, 768, 1024, 2048, 3072, 4096, 4608


def _pick(n, prefs):
    for p in prefs:
        if n % p == 0:
            return p
    return n


def _params(*sem):
    return pltpu.CompilerParams(dimension_semantics=sem, vmem_limit_bytes=VMEM_LIMIT_V7X)


def _mod_kernel(c_ref, w_ref, b_ref, o_ref):
    c = c_ref[...]
    s = c * (1.0 / (1.0 + jnp.exp(-c)))
    o_ref[0] = jnp.dot(s.astype(BF16), w_ref[0].astype(BF16), preferred_element_type=F32) + b_ref[0]


def _modulation(cz, w_mod, b_mod):
    depth, d, n = w_mod.shape
    tn = _pick(n, (1024, 768, 512, 256, 128))
    return pl.pallas_call(
        _mod_kernel,
        grid=(depth, n // tn),
        in_specs=[
            pl.BlockSpec((8, d), lambda l, j: (0, 0)),
            pl.BlockSpec((1, d, tn), lambda l, j: (l, 0, j)),
            pl.BlockSpec((1, 1, tn), lambda l, j: (l, 0, j)),
        ],
        out_specs=pl.BlockSpec((1, 8, tn), lambda l, j: (l, 0, j)),
        out_shape=jax.ShapeDtypeStruct((depth, 8, n), F32),
        compiler_params=_params("parallel", "parallel"),
        name="modulation",
    )(cz, w_mod, b_mod.reshape(depth, 1, n))


def _nmm_kernel(*refs, has_mod, emit_h):
    if has_mod:
        x_ref, g_ref, sh_ref, sc_ref, w_ref = refs[:5]
        rest = refs[5:]
    else:
        x_ref, g_ref, w_ref = refs[:3]
        rest = refs[3:]
    if emit_h:
        o_ref, ho_ref, h_scr = rest
    else:
        o_ref, h_scr = rest

    @pl.when(pl.program_id(1) == 0)
    def _():
        x = x_ref[...].astype(F32)
        y = x * lax.rsqrt(jnp.mean(x * x, axis=-1, keepdims=True) + EPS) * g_ref[...]
        if has_mod:
            y = y * (1.0 + sc_ref[0]) + sh_ref[0]
        hb = y.astype(BF16)
        h_scr[...] = hb
        if emit_h:
            ho_ref[...] = hb

    o_ref[...] = jnp.dot(h_scr[...], w_ref[...], preferred_element_type=F32).astype(o_ref.dtype)


def _nmm(x, g, w, *, k_block=0, shift=None, scale=None, emit_h=False, out_dtype=F32, name):
    n = x.shape[0]
    k, m = w.shape
    has_mod = shift is not None
    n_mod = shift.shape[0] if has_mod else 1
    per_mod = n // n_mod
    tm = _pick(per_mod, (1024, 512, 256, 128))
    tn = _pick(m, (512, 768, 256, 128))
    tiles_per_mod = per_mod // tm
    in_specs = [pl.BlockSpec((tm, k), lambda i, j: (i, k_block)), pl.BlockSpec((1, k), lambda i, j: (0, 0))]
    args = [x, g.reshape(1, k)]
    if has_mod:
        mod_spec = pl.BlockSpec((1, 1, k), lambda i, j: (i // tiles_per_mod, 0, 0))
        in_specs += [mod_spec, mod_spec]
        args += [shift, scale]
    in_specs.append(pl.BlockSpec((k, tn), lambda i, j: (0, j)))
    args.append(w)
    out_specs = pl.BlockSpec((tm, tn), lambda i, j: (i, j))
    out_shape = jax.ShapeDtypeStruct((n, m), out_dtype)
    if emit_h:
        out_specs = [out_specs, pl.BlockSpec((tm, k), lambda i, j: (i, 0))]
        out_shape = [out_shape, jax.ShapeDtypeStruct((n, k), BF16)]
    return pl.pallas_call(
        functools.partial(_nmm_kernel, has_mod=has_mod, emit_h=emit_h),
        grid=(n // tm, m // tn),
        in_specs=in_specs,
        out_specs=out_specs,
        out_shape=out_shape,
        scratch_shapes=[pltpu.VMEM((tm, k), BF16)],
        compiler_params=_params("parallel", "arbitrary"),
        name=name,
    )(*args)


def _mm_res_kernel(a_ref, w_ref, res_ref, gate_ref, o_ref):
    acc = jnp.dot(a_ref[...], w_ref[...], preferred_element_type=F32)
    o_ref[...] = res_ref[...] + gate_ref[0] * acc


def _mm_res(a, w, res, gate, *, name):
    n, k = a.shape
    m = w.shape[1]
    per_mod = n // gate.shape[0]
    tm = _pick(per_mod, (1024, 512, 256, 128))
    tn = _pick(m, (512, 256, 128))
    tiles_per_mod = per_mod // tm
    return pl.pallas_call(
        _mm_res_kernel,
        grid=(n // tm, m // tn),
        in_specs=[
            pl.BlockSpec((tm, k), lambda i, j: (i, 0)),
            pl.BlockSpec((k, tn), lambda i, j: (0, j)),
            pl.BlockSpec((tm, tn), lambda i, j: (i, j)),
            pl.BlockSpec((1, 1, tn), lambda i, j: (i // tiles_per_mod, 0, j)),
        ],
        out_specs=pl.BlockSpec((tm, tn), lambda i, j: (i, j)),
        out_shape=jax.ShapeDtypeStruct((n, m), F32),
        compiler_params=_params("parallel", "parallel"),
        name=name,
    )(a, w, res, gate)


def _rope(y, cos, sin_up, sin_dn, n_freq):
    w = y.shape[-1]
    return y * cos + pltpu.roll(y, w - n_freq, 1) * sin_up + pltpu.roll(y, n_freq, 1) * sin_dn


def _sumsq(x):
    return jnp.sum(x * x, axis=-1, keepdims=True)


def _prep_aq_kernel(*refs, rope, scale):
    if rope:
        x_ref, gn_ref, gr_ref, cos_ref, su_ref, sd_ref, o_ref = refs
    else:
        x_ref, gn_ref, gr_ref, o_ref = refs
    gn, gr = gn_ref[...], gr_ref[...]
    for h in range(A_HEADS):
        xn = x_ref[:, h * 128:(h + 1) * 128]
        xr = x_ref[:, 1024 + h * 128:1024 + (h + 1) * 128]
        r = lax.rsqrt((_sumsq(xn) + _sumsq(xr)) * (1.0 / A_QK) + EPS)
        yn = xn * r * gn
        yr = xr * r * gr
        if rope:
            yr = _rope(yr, cos_ref[...], su_ref[...], sd_ref[...], A_ROPE // 4)
        o_ref[0, h, 0, :, 0:128] = (yn * scale).astype(BF16)
        o_ref[0, h, 0, :, 128:256] = (yr * scale).astype(BF16)


def _prep_akv_kernel(*refs, rope):
    if rope:
        x_ref, kr_ref, gn_ref, gr_ref, cos_ref, su_ref, sd_ref, k_ref, v_ref = refs
    else:
        x_ref, kr_ref, gn_ref, gr_ref, k_ref, v_ref = refs
    gn = gn_ref[...]
    kr = kr_ref[...]
    ss_r = _sumsq(kr)
    krg = kr * gr_ref[...]
    if rope:
        krg = _rope(krg, cos_ref[...], su_ref[...], sd_ref[...], A_ROPE // 4)
    for h in range(A_HEADS):
        xn = x_ref[:, h * 256:h * 256 + 128]
        r = lax.rsqrt((_sumsq(xn) + ss_r) * (1.0 / A_QK) + EPS)
        k_ref[0, h, :, 0:128] = (xn * r * gn).astype(BF16)
        k_ref[0, h, :, 128:256] = (krg * r).astype(BF16)
        v_ref[0, h] = x_ref[:, h * 256 + 128:(h + 1) * 256].astype(BF16)


def _norm64_pairs(x, g):
    x2 = x * x
    lo = lax.broadcasted_iota(jnp.int32, x.shape, 1) < B_DIM
    s_lo = jnp.sum(jnp.where(lo, x2, 0.0), axis=-1, keepdims=True)
    s_hi = jnp.sum(jnp.where(lo, 0.0, x2), axis=-1, keepdims=True)
    r = jnp.where(lo, lax.rsqrt(s_lo * (1.0 / B_DIM) + EPS), lax.rsqrt(s_hi * (1.0 / B_DIM) + EPS))
    return x * r * g


def _prep_b_kernel(*refs, rope, scale):
    if rope:
        q_ref, k_ref, v_ref, gq_ref, gk_ref, cos_ref, su_ref, sd_ref, qo_ref, ko_ref, vo_ref = refs
    else:
        q_ref, k_ref, v_ref, gq_ref, gk_ref, qo_ref, ko_ref, vo_ref = refs
    for src, g_ref, dst, mul in ((q_ref, gq_ref, qo_ref, scale), (k_ref, gk_ref, ko_ref, None)):
        g = g_ref[...]
        for j in range(B_HEADS):
            y = _norm64_pairs(src[:, j * 128:(j + 1) * 128], g)
            if rope:
                y = _rope(y, cos_ref[...], su_ref[...], sd_ref[...], B_DIM // 4)
            if mul is not None:
                y = y * mul
            yb = y.astype(BF16)
            if dst is qo_ref:
                dst[0, 2 * j, 0] = yb[:, 0:B_DIM]
                dst[0, 2 * j + 1, 0] = yb[:, B_DIM:2 * B_DIM]
            else:
                dst[0, 2 * j] = yb[:, 0:B_DIM]
                dst[0, 2 * j + 1] = yb[:, B_DIM:2 * B_DIM]
    for h in range(B_HEADS):
        vo_ref[0, h] = v_ref[:, h * B_V:(h + 1) * B_V].astype(BF16)


def _prep_c_kernel(*refs, rope, scale):
    if rope:
        q_ref, kv_ref, gq_ref, gk_ref, cos_ref, su_ref, sd_ref, qo_ref, ko_ref, vo_ref = refs
    else:
        q_ref, kv_ref, gq_ref, gk_ref, qo_ref, ko_ref, vo_ref = refs

    def norm_rope(x, g):
        y = x * lax.rsqrt(_sumsq(x) * (1.0 / C_DIM) + EPS) * g
        if rope:
            y = _rope(y, cos_ref[...], su_ref[...], sd_ref[...], C_DIM // 4)
        return y

    gq, gk = gq_ref[...], gk_ref[...]
    for h in range(C_HEADS):
        y = norm_rope(q_ref[:, h * C_DIM:(h + 1) * C_DIM], gq) * scale
        qo_ref[0, h // C_GROUP, h % C_GROUP] = y.astype(BF16)
    for h in range(C_KV_HEADS):
        ko_ref[0, h] = norm_rope(kv_ref[:, h * C_DIM:(h + 1) * C_DIM], gk).astype(BF16)
        vo_ref[0, h] = kv_ref[:, (C_KV_HEADS + h) * C_DIM:(C_KV_HEADS + h + 1) * C_DIM].astype(BF16)


def _prep_call(kernel, ins, in_blocks, consts, tables, outs, *, batch, seq, name):
    ts = _pick(seq, (256, 128))
    n_s = seq // ts
    in_specs = [pl.BlockSpec((ts, w), functools.partial(lambda b, i, c: (b * n_s + i, c), c=c)) for w, c in in_blocks]
    in_specs += [pl.BlockSpec(c.shape, lambda b, i: (0, 0)) for c in consts]
    if tables is not None:
        in_specs += [pl.BlockSpec((ts, LANE_V7X), lambda b, i: (i, 0)) for _ in tables]
    out_specs, out_shape = [], []
    for heads, d in outs:
        nz = len(heads)
        out_specs.append(pl.BlockSpec((1,) + heads + (ts, d),
                                      functools.partial(lambda b, i, nz: (b,) + (0,) * nz + (i, 0), nz=nz)))
        out_shape.append(jax.ShapeDtypeStruct((batch,) + heads + (seq, d), BF16))
    args = list(ins) + list(consts) + (list(tables) if tables is not None else [])
    return pl.pallas_call(
        kernel,
        grid=(batch, n_s),
        in_specs=in_specs,
        out_specs=out_specs,
        out_shape=out_shape,
        compiler_params=_params("parallel", "parallel"),
        name=name,
    )(*args)


def _flash_kernel(q_ref, k_ref, v_ref, o_ref, m_scr, l_scr, acc_scr, *, group, tq, tk, n_sub):
    kv = pl.program_id(3)
    dv = LANE_V7X

    @pl.when(kv == 0)
    def _():
        m_scr[...] = jnp.full(m_scr.shape, NEG_INF, F32)
        l_scr[...] = jnp.zeros(l_scr.shape, F32)
        acc_scr[...] = jnp.zeros(acc_scr.shape, F32)

    q = q_ref[0, 0].reshape(group * tq, q_ref.shape[-1])
    ones = jnp.ones((tk, LANE_V7X), BF16)
    for t in range(n_sub):
        k = k_ref[0, 0, t * tk:(t + 1) * tk, :]
        v1 = jnp.concatenate([v_ref[0, 0, t * tk:(t + 1) * tk, :], ones], axis=1)
        s = lax.dot_general(q, k, (((1,), (1,)), ((), ())), preferred_element_type=F32)
        m_prev = m_scr[...]
        m_new = jnp.maximum(m_prev, jnp.max(s, axis=-1, keepdims=True))
        alpha = jnp.exp2(m_prev - m_new)
        p = jnp.exp2(s - jnp.concatenate([m_new] * (tk // LANE_V7X), axis=1))
        pv = jnp.dot(p.astype(BF16), v1, preferred_element_type=F32)
        acc_scr[...] = alpha * acc_scr[...] + pv[:, :dv]
        l_scr[...] = alpha * l_scr[...] + pv[:, dv:]
        m_scr[...] = m_new

    @pl.when(kv == pl.num_programs(3) - 1)
    def _():
        out = acc_scr[...] / l_scr[...]
        for g in range(group):
            o_ref[0, :, g * dv:(g + 1) * dv] = out[g * tq:(g + 1) * tq].astype(o_ref.dtype)


def _flash(q, k, v, *, out_dtype, name, rows_pref=512, tk_pref=256, sub_pref=33):
    b, hk, group, s, d = q.shape
    t = k.shape[2]
    hv, dv = v.shape[1], v.shape[3]
    assert dv == LANE_V7X
    k_per_v = hk // hv
    tq = _pick(s, (rows_pref // group, 256 // group, 128 // group))
    tk = _pick(t, (tk_pref, 256, 128))
    n_sub = _pick(t // tk, (sub_pref, 11, 8, 4, 3, 2, 1))
    rows = group * tq
    stat = pltpu.VMEM((rows, LANE_V7X), F32)
    return pl.pallas_call(
        functools.partial(_flash_kernel, group=group, tq=tq, tk=tk, n_sub=n_sub),
        grid=(b, hk, s // tq, t // (tk * n_sub)),
        in_specs=[
            pl.BlockSpec((1, 1, group, tq, d), lambda bi, h, i, j: (bi, h, 0, i, 0)),
            pl.BlockSpec((1, 1, tk * n_sub, d), lambda bi, h, i, j: (bi, h, j, 0)),
            pl.BlockSpec((1, 1, tk * n_sub, dv), lambda bi, h, i, j: (bi, h // k_per_v, j, 0)),
        ],
        out_specs=pl.BlockSpec((1, tq, group * dv), lambda bi, h, i, j: (bi, i, h)),
        out_shape=jax.ShapeDtypeStruct((b, s, hk * group * dv), out_dtype),
        scratch_shapes=[stat, stat, stat],
        compiler_params=_params("parallel", "parallel", "parallel", "arbitrary"),
        name=name,
    )(q, k, v)


def _merge_ab_kernel(oa_ref, ob_ref, g_ref, lv_ref, o_ref, *, lam_init):
    lv = lv_ref[...]
    lam = (jnp.exp(jnp.sum(lv[0:1] * lv[1:2], axis=-1, keepdims=True))
           - jnp.exp(jnp.sum(lv[2:3] * lv[3:4], axis=-1, keepdims=True)) + lam_init)
    n_a = A_HEADS * A_V
    o_ref[:, 0:n_a] = oa_ref[...].astype(BF16)
    g = g_ref[...]
    for h in range(B_HEADS):
        o = ob_ref[:, 2 * h * B_V:(2 * h + 1) * B_V] - lam * ob_ref[:, (2 * h + 1) * B_V:(2 * h + 2) * B_V]
        y = o * lax.rsqrt(_sumsq(o) * (1.0 / B_V) + EPS) * g * (1.0 - lam_init)
        o_ref[:, n_a + h * B_V:n_a + (h + 1) * B_V] = y.astype(BF16)


def _merge_ab(oa, ob, g_sub, lam_vec, lam_init):
    n = oa.shape[0]
    tm = _pick(n, (512, 256, 128))
    wa, wb = oa.shape[1], ob.shape[1]
    wo = wa + B_HEADS * B_V
    return pl.pallas_call(
        functools.partial(_merge_ab_kernel, lam_init=lam_init),
        grid=(n // tm,),
        in_specs=[
            pl.BlockSpec((tm, wa), lambda i: (i, 0)),
            pl.BlockSpec((tm, wb), lambda i: (i, 0)),
            pl.BlockSpec((1, B_V), lambda i: (0, 0)),
            pl.BlockSpec((4, B_DIM), lambda i: (0, 0)),
        ],
        out_specs=pl.BlockSpec((tm, wo), lambda i: (i, 0)),
        out_shape=jax.ShapeDtypeStruct((n, wo), BF16),
        compiler_params=_params("parallel"),
        name="merge_ab",
    )(oa, ob, g_sub.reshape(1, B_V), lam_vec)


def _top_rows(s, k):
    vals = []
    cur = s
    for _ in range(k):
        mx = jnp.max(cur, axis=0, keepdims=True)
        vals.append(mx)
        cur = jnp.where(cur == mx, NEG_INF, cur)
    return jnp.concatenate(vals, axis=0)


def _peer_stats_kernel(q_ref, keys_ref, c1_ref, e2_ref, th_ref):
    dn = (((1,), (1,)), ((), ()))
    s1 = lax.dot_general(keys_ref[0], q_ref[:, 0:N_KEYS].astype(BF16), dn, preferred_element_type=F32)
    s2 = lax.dot_general(keys_ref[1], q_ref[:, N_KEYS:2 * N_KEYS].astype(BF16), dn, preferred_element_type=F32)
    v1 = _top_rows(s1, P_TOPK)
    v2 = _top_rows(s2, P_TOPK)
    n_b = [P_TOPK // (a + 1) for a in range(P_TOPK)]
    cand = jnp.concatenate([v1[a:a + 1] + v2[0:n_b[a]] for a in range(P_TOPK)], axis=0)
    best = _top_rows(cand, P_TOPK)
    inv_z = 1.0 / jnp.sum(jnp.exp(best - best[0:1]), axis=0, keepdims=True)
    c1_ref[0] = jnp.exp(s1 - v1[0:1]) * inv_z
    e2_ref[0] = jnp.exp(s2 - v2[0:1])
    c1v = jnp.exp(v1 - v1[0:1]) * inv_z
    e2v = jnp.exp(v2 - v2[0:1])
    gates = jnp.concatenate([c1v[a:a + 1] * e2v[0:n_b[a]] for a in range(P_TOPK)], axis=0)
    th_ref[0] = jnp.min(jnp.where(cand >= best[P_TOPK - 1:P_TOPK], gates, jnp.inf), axis=0, keepdims=True)


def _peer_stats(q, keys):
    n = q.shape[0]
    tm = _pick(n, (256, 128))
    big = pl.BlockSpec((1, N_KEYS, tm), lambda i, h: (h, 0, i))
    big_shape = jax.ShapeDtypeStruct((P_HEADS, N_KEYS, n), F32)
    return pl.pallas_call(
        _peer_stats_kernel,
        grid=(n // tm, P_HEADS),
        in_specs=[
            pl.BlockSpec((tm, P_DQ), lambda i, h: (i, h)),
            pl.BlockSpec((2, N_KEYS, P_DQ // 2), lambda i, h: (h, 0, 0)),
        ],
        out_specs=[big, big, pl.BlockSpec((1, 1, tm), lambda i, h: (h, 0, i))],
        out_shape=[big_shape, big_shape, jax.ShapeDtypeStruct((P_HEADS, 1, n), F32)],
        compiler_params=_params("parallel", "parallel"),
        name="peer_stats",
    )(q, keys)


def _gelu_tanh(a):
    return 0.5 * a * (1.0 + jnp.tanh(math.sqrt(2.0 / math.pi) * (a + 0.044715 * (a * a * a))))


def _peer_dense_kernel(h_ref, u_ref, vt_ref, c1_ref, e2_ref, th_ref, res_ref, gate_ref,
                       o_ref, acc_ref, at_ref, wt_ref, *, te, tm, tc):
    j = pl.program_id(1)

    @pl.when(j == 0)
    def _():
        acc_ref[...] = jnp.zeros(acc_ref.shape, F32)

    at_ref[...] = lax.dot_general(u_ref[...], h_ref[...], (((1,), (1,)), ((), ())), preferred_element_type=F32)
    for i1 in range(te // N_KEYS):
        rows = slice(i1 * N_KEYS, (i1 + 1) * N_KEYS)
        for c in range(tm // tc):
            cols = slice(c * tc, (c + 1) * tc)
            m = jnp.zeros((N_KEYS, tc), F32)
            for h in range(P_HEADS):
                g = e2_ref[h, :, cols] * c1_ref[h, i1:i1 + 1, cols]
                m = m + jnp.where(g >= th_ref[h, :, cols], g, 0.0)
            wt_ref[rows, cols] = (m * _gelu_tanh(at_ref[rows, cols])).astype(BF16)
    acc_ref[...] += jnp.dot(vt_ref[...], wt_ref[...], preferred_element_type=F32)

    @pl.when(j == pl.num_programs(1) - 1)
    def _():
        o_ref[...] = res_ref[...] + gate_ref[0] * acc_ref[...].T


def _peer_dense(h, u, vt, stats, res, gate):
    n, d = h.shape
    e = u.shape[0]
    c1, e2, theta = stats
    per_mod = n // gate.shape[0]
    tm = _pick(per_mod, (512, 256, 128))
    te = _pick(e, (1024,))
    tc = _pick(tm, (256, 128))
    tiles_per_mod = per_mod // tm
    once = dict(pipeline_mode=pl.Buffered(1))
    big = pl.BlockSpec((P_HEADS, N_KEYS, tm), lambda i, j: (0, 0, i), **once)
    row = pl.BlockSpec((P_HEADS, te // N_KEYS, tm), lambda i, j: (0, j, i))
    return pl.pallas_call(
        functools.partial(_peer_dense_kernel, te=te, tm=tm, tc=tc),
        grid=(n // tm, e // te),
        in_specs=[
            pl.BlockSpec((tm, d), lambda i, j: (i, 0)),
            pl.BlockSpec((te, d), lambda i, j: (j, 0)),
            pl.BlockSpec((d, te), lambda i, j: (0, j)),
            row, big,
            pl.BlockSpec((P_HEADS, 1, tm), lambda i, j: (0, 0, i)),
            pl.BlockSpec((tm, d), lambda i, j: (i, 0), **once),
            pl.BlockSpec((1, 1, d), lambda i, j: (i // tiles_per_mod, 0, 0)),
        ],
        out_specs=pl.BlockSpec((tm, d), lambda i, j: (i, 0), **once),
        out_shape=jax.ShapeDtypeStruct((n, d), F32),
        scratch_shapes=[pltpu.VMEM((d, tm), F32), pltpu.VMEM((te, tm), F32), pltpu.VMEM((te, tm), BF16)],
        compiler_params=_params("parallel", "arbitrary"),
        name="peer_dense",
    )(h, u, vt, c1, e2, theta, res, gate)


def _rope_tables(seq, rot_dim):
    n_freq = rot_dim // 4
    freqs = ROPE_BASE ** (-jnp.arange(n_freq, dtype=F32) / n_freq)
    pos = jnp.arange(seq)
    row = (pos // GRID_W).astype(F32)[:, None] * freqs
    col = (pos % GRID_W).astype(F32)[:, None] * freqs
    zero = jnp.zeros_like(row)
    cos = jnp.concatenate([jnp.cos(row), jnp.cos(row), jnp.cos(col), jnp.cos(col)], axis=-1)
    s_up = jnp.concatenate([-jnp.sin(row), zero, -jnp.sin(col), zero], axis=-1)
    s_dn = jnp.concatenate([zero, jnp.sin(row), zero, jnp.sin(col)], axis=-1)
    return cos, s_up, s_dn


def _tile_tables(tabs, reps, width):
    out = []
    for idx, t in enumerate(tabs):
        t = jnp.tile(t, (1, reps))
        pad = width - t.shape[1]
        if pad:
            fill = jnp.ones if idx == 0 else jnp.zeros
            t = jnp.concatenate([t, fill((t.shape[0], pad), F32)], axis=-1)
        out.append(t)
    return out


def _pad_lanes(v, width):
    return jnp.pad(v, (0, width - v.shape[0])).reshape(1, width)


def _layout_w_in_ab(w):
    d = w.shape[0]
    cq, ckv, kr, bq, bk, bv = jnp.split(w, [768, 1280, 1344, 2368, 3392], axis=1)
    return jnp.concatenate([cq, kr, jnp.zeros((d, AB_BQ - AB_KR - A_ROPE), w.dtype), bq, bk, bv, ckv],
                           axis=1).astype(BF16)


def _layout_w_uq(w):
    k = w.shape[0]
    w3 = w.reshape(k, A_HEADS, A_QK)
    nope = w3[:, :, :A_NOPE].reshape(k, A_HEADS * A_NOPE)
    rope = jnp.pad(w3[:, :, A_NOPE:], ((0, 0), (0, 0), (0, LANE_V7X - A_ROPE))).reshape(k, A_HEADS * LANE_V7X)
    return jnp.concatenate([nope, rope], axis=1).astype(BF16)


def _mixer_ab(tok, mod, p, tables, *, batch, seq, rope):
    proj = _nmm(tok, p["g_norm1"], p["w_in"], shift=mod[0], scale=mod[1], name="ab_in")
    qa_raw = _nmm(proj, p["g_cq"], p["w_uq"], k_block=AB_CQ // A_Q_LORA, name="ab_uq")
    kva_raw = _nmm(proj, p["g_ckv"], p["w_ukv"], k_block=AB_CKV // A_KV_LORA, name="ab_ukv")
    sa, sb = A_QK ** -0.5 * LOG2_E, B_DIM ** -0.5 * LOG2_E
    (qa,) = _prep_call(
        functools.partial(_prep_aq_kernel, rope=rope, scale=sa), [qa_raw], [(qa_raw.shape[1], 0)],
        [p["gqa_n"], p["gqa_r"]], tables["a"] if rope else None, [((A_HEADS, 1), 2 * LANE_V7X)],
        batch=batch, seq=seq, name="prep_aq")
    ka, va = _prep_call(
        functools.partial(_prep_akv_kernel, rope=rope), [kva_raw, proj],
        [(kva_raw.shape[1], 0), (LANE_V7X, AB_KR // LANE_V7X)],
        [p["gka_n"], p["gka_r"]], tables["a"] if rope else None,
        [((A_HEADS,), 2 * LANE_V7X), ((A_HEADS,), A_V)], batch=batch, seq=seq, name="prep_akv")
    wb = B_HEADS * 2 * B_DIM
    qb, kb, vb = _prep_call(
        functools.partial(_prep_b_kernel, rope=rope, scale=sb), [proj, proj, proj],
        [(wb, AB_BQ // wb), (wb, AB_BK // wb), (wb, AB_BV // wb)],
        [p["gqb"], p["gkb"]], tables["b"] if rope else None,
        [((2 * B_HEADS, 1), B_DIM), ((2 * B_HEADS,), B_DIM), ((B_HEADS,), B_V)],
        batch=batch, seq=seq, name="prep_b")
    return (qa, ka, va), (qb, kb, vb)


def _mixer_c(tok, mod, p, tables, *, batch, seq, rope):
    proj = _nmm(tok, p["g_norm1"], p["w_in"], shift=mod[0], scale=mod[1], name="c_in")
    wq = C_HEADS * C_DIM
    wkv = 2 * C_KV_HEADS * C_DIM
    return _prep_call(
        functools.partial(_prep_c_kernel, rope=rope, scale=C_DIM ** -0.5 * LOG2_E), [proj, proj],
        [(wq, 0), (wkv, wq // wkv)], [p["gq"], p["gk"]], tables["c"] if rope else None,
        [((C_KV_HEADS, C_GROUP), C_DIM), ((C_KV_HEADS,), C_DIM), ((C_KV_HEADS,), C_DIM)],
        batch=batch, seq=seq, name="prep_c")


def _cat_kv(a, b):
    return jnp.concatenate([a, b], axis=2)


def _peer(tok, mod, gate, p, name):
    q, h = _nmm(tok, p["g_norm2"], p["w_pq"], shift=mod[0], scale=mod[1], emit_h=True, name=name)
    stats = _peer_stats(q, p["keys"])
    return _peer_dense(h, p["u"], p["vt"], stats, tok, gate)


def kernel(x, c, ctx, c_ctx, w_mod, b_mod, g_norm1, g_norm2, w_in_ab, g_cq, w_uq, g_ckv, w_ukv, g_qn_a, g_kn_a,
           lam_vec, g_qn_b, g_kn_b, g_sub_b, w_out_ab, w_in_c, g_qn_c, g_kn_c, w_out_c, w_pq, sub_keys,
           expert_u, expert_v):
    batch, seq, d = x.shape
    ctx_len = ctx.shape[1]
    depth = w_mod.shape[0]
    assert batch + 1 <= 8

    cz = jnp.concatenate([c, c_ctx[None, :], jnp.zeros((8 - batch - 1, d), F32)], axis=0)
    mod_all = _modulation(cz, w_mod, b_mod).reshape(depth, 8, 6, d)

    tab64 = _rope_tables(seq, A_ROPE)
    tables = {
        "a": _tile_tables(tab64, 1, LANE_V7X),
        "b": _tile_tables(tab64, 2, LANE_V7X),
        "c": _rope_tables(seq, C_DIM),
    }

    xt = x.reshape(batch * seq, d)
    zt = ctx.reshape(batch * ctx_len, d)

    for layer in range(depth):
        last = layer == depth - 1
        e = layer // 2
        mx = [mod_all[layer, :batch, i][:, None, :] for i in range(6)]
        mz = [mod_all[layer, batch:batch + 1, i][:, None, :] for i in range(6)]
        pp = {
            "g_norm2": g_norm2[layer], "w_pq": w_pq[layer].astype(BF16),
            "keys": sub_keys[layer].reshape(2 * P_HEADS, N_KEYS, P_DQ // 2).astype(BF16),
            "u": expert_u[layer].astype(BF16), "vt": expert_v[layer].T.astype(BF16),
        }
        if layer % 2 == 0:
            lam_init = 0.8 - 0.6 * math.exp(-0.3 * layer)
            p = {
                "g_norm1": g_norm1[layer], "w_in": _layout_w_in_ab(w_in_ab[e]),
                "g_cq": g_cq[e], "w_uq": _layout_w_uq(w_uq[e]),
                "g_ckv": g_ckv[e], "w_ukv": w_ukv[e].astype(BF16),
                "gqa_n": g_qn_a[e][:A_NOPE].reshape(1, A_NOPE), "gqa_r": _pad_lanes(g_qn_a[e][A_NOPE:], LANE_V7X),
                "gka_n": g_kn_a[e][:A_NOPE].reshape(1, A_NOPE), "gka_r": _pad_lanes(g_kn_a[e][A_NOPE:], LANE_V7X),
                "gqb": jnp.tile(g_qn_b[e], 2).reshape(1, 2 * B_DIM), "gkb": jnp.tile(g_kn_b[e], 2).reshape(1, 2 * B_DIM),
            }
            w_out = w_out_ab[e].astype(BF16)
            (qa, ka, va), (qb, kb, vb) = _mixer_ab(xt, mx, p, tables, batch=batch, seq=seq, rope=True)
            (qaz, kaz, vaz), (qbz, kbz, vbz) = _mixer_ab(zt, mz, p, tables, batch=batch, seq=ctx_len, rope=False)
            oa = _flash(qa, _cat_kv(ka, kaz), _cat_kv(va, vaz), out_dtype=BF16, name="flash_a")
            ob = _flash(qb, _cat_kv(kb, kbz), _cat_kv(vb, vbz), out_dtype=F32, name="flash_b")
            att = _merge_ab(oa.reshape(batch * seq, -1), ob.reshape(batch * seq, -1), g_sub_b[e], lam_vec[e], lam_init)
            if not last:
                oaz = _flash(qaz, kaz, vaz, out_dtype=BF16, name="flash_az")
                obz = _flash(qbz, kbz, vbz, out_dtype=F32, name="flash_bz")
                attz = _merge_ab(oaz.reshape(batch * ctx_len, -1), obz.reshape(batch * ctx_len, -1),
                                 g_sub_b[e], lam_vec[e], lam_init)
        else:
            p = {
                "g_norm1": g_norm1[layer], "w_in": w_in_c[e].astype(BF16),
                "gq": g_qn_c[e].reshape(1, C_DIM), "gk": g_kn_c[e].reshape(1, C_DIM),
            }
            w_out = w_out_c[e].astype(BF16)
            q, k, v = _mixer_c(xt, mx, p, tables, batch=batch, seq=seq, rope=True)
            qz, kz, vz = _mixer_c(zt, mz, p, tables, batch=batch, seq=ctx_len, rope=False)
            att = _flash(q, _cat_kv(k, kz), _cat_kv(v, vz), out_dtype=BF16, name="flash_c").reshape(batch * seq, -1)
            if not last:
                attz = _flash(qz, kz, vz, out_dtype=BF16, name="flash_cz").reshape(batch * ctx_len, -1)

        xt = _mm_res(att, w_out, xt, mx[2], name="out_proj")
        xt = _peer(xt, (mx[3], mx[4]), mx[5], pp, "peer_q")
        if not last:
            zt = _mm_res(attz, w_out, zt, mz[2], name="out_proj_z")
            zt = _peer(zt, (mz[3], mz[4]), mz[5], pp, "peer_qz")
    return xt.reshape(batch, seq, d)
```

```python
import functools
import math

import jax
import jax.numpy as jnp
from jax import lax
from jax.experimental import pallas as pl
from jax.experimental.pallas import tpu as pltpu

F32 = jnp.float32
BF16 = jnp.bfloat16

LANE_V7X = 128
VMEM_LIMIT_V7X = 56 * 1024 * 1024

GRID_W = 64
ROPE_BASE = 10000.0
EPS = 1e-6
A_HEADS, A_Q_LORA, A_KV_LORA, A_NOPE, A_ROPE, A_V = 8, 768, 512, 128, 64, 128
A_QK = A_NOPE + A_ROPE
B_HEADS, B_DIM, B_V = 8, 64, 128
C_HEADS, C_KV_HEADS, C_DIM = 16, 4, 128
C_GROUP = C_HEADS // C_KV_HEADS
P_HEADS, N_KEYS, P_DQ, P_TOPK = 8, 128, 256, 16
N_EXPERTS = N_KEYS * N_KEYS
NEG_INF = float("-inf")
LOG2_E = math.log2(math.e)

AB_CQ, AB_KR, AB_BQ, AB_BK, AB_BV, AB_CKV, AB_COLS = 0, 768, 1024, 2048, 3072, 4096, 4608


def _pick(n, prefs):
    for p in prefs:
        if n % p == 0:
            return p
    return n


def _params(*sem):
    return pltpu.CompilerParams(dimension_semantics=sem, vmem_limit_bytes=VMEM_LIMIT_V7X)


def _mod_kernel(c_ref, w_ref, b_ref, o_ref):
    c = c_ref[...]
    s = c * (1.0 / (1.0 + jnp.exp(-c)))
    o_ref[0] = jnp.dot(s.astype(BF16), w_ref[0].astype(BF16), preferred_element_type=F32) + b_ref[0]


def _modulation(cz, w_mod, b_mod):
    depth, d, n = w_mod.shape
    tn = _pick(n, (1024, 768, 512, 256, 128))
    return pl.pallas_call(
        _mod_kernel,
        grid=(depth, n // tn),
        in_specs=[
            pl.BlockSpec((8, d), lambda l, j: (0, 0)),
            pl.BlockSpec((1, d, tn), lambda l, j: (l, 0, j)),
            pl.BlockSpec((1, 1, tn), lambda l, j: (l, 0, j)),
        ],
        out_specs=pl.BlockSpec((1, 8, tn), lambda l, j: (l, 0, j)),
        out_shape=jax.ShapeDtypeStruct((depth, 8, n), F32),
        compiler_params=_params("parallel", "parallel"),
        name="modulation",
    )(cz, w_mod, b_mod.reshape(depth, 1, n))


def _nmm_kernel(*refs, has_mod, emit_h):
    if has_mod:
        x_ref, g_ref, sh_ref, sc_ref, w_ref = refs[:5]
        rest = refs[5:]
    else:
        x_ref, g_ref, w_ref = refs[:3]
        rest = refs[3:]
    if emit_h:
        o_ref, ho_ref, h_scr = rest
    else:
        o_ref, h_scr = rest

    @pl.when(pl.program_id(1) == 0)
    def _():
        x = x_ref[...].astype(F32)
        y = x * lax.rsqrt(jnp.mean(x * x, axis=-1, keepdims=True) + EPS) * g_ref[...]
        if has_mod:
            y = y * (1.0 + sc_ref[0]) + sh_ref[0]
        hb = y.astype(BF16)
        h_scr[...] = hb
        if emit_h:
            ho_ref[...] = hb

    o_ref[...] = jnp.dot(h_scr[...], w_ref[...], preferred_element_type=F32).astype(o_ref.dtype)


def _nmm(x, g, w, *, k_block=0, shift=None, scale=None, emit_h=False, out_dtype=F32, name):
    n = x.shape[0]
    k, m = w.shape
    has_mod = shift is not None
    n_mod = shift.shape[0] if has_mod else 1
    per_mod = n // n_mod
    tm = _pick(per_mod, (1024, 512, 256, 128))
    tn = _pick(m, (512, 768, 256, 128))
    tiles_per_mod = per_mod // tm
    in_specs = [pl.BlockSpec((tm, k), lambda i, j: (i, k_block)), pl.BlockSpec((1, k), lambda i, j: (0, 0))]
    args = [x, g.reshape(1, k)]
    if has_mod:
        mod_spec = pl.BlockSpec((1, 1, k), lambda i, j: (i // tiles_per_mod, 0, 0))
        in_specs += [mod_spec, mod_spec]
        args += [shift, scale]
    in_specs.append(pl.BlockSpec((k, tn), lambda i, j: (0, j)))
    args.append(w)
    out_specs = pl.BlockSpec((tm, tn), lambda i, j: (i, j))
    out_shape = jax.ShapeDtypeStruct((n, m), out_dtype)
    if emit_h:
        out_specs = [out_specs, pl.BlockSpec((tm, k), lambda i, j: (i, 0))]
        out_shape = [out_shape, jax.ShapeDtypeStruct((n, k), BF16)]
    return pl.pallas_call(
        functools.partial(_nmm_kernel, has_mod=has_mod, emit_h=emit_h),
        grid=(n // tm, m // tn),
        in_specs=in_specs,
        out_specs=out_specs,
        out_shape=out_shape,
        scratch_shapes=[pltpu.VMEM((tm, k), BF16)],
        compiler_params=_params("parallel", "arbitrary"),
        name=name,
    )(*args)


def _mm_res_kernel(a_ref, w_ref, res_ref, gate_ref, o_ref):
    acc = jnp.dot(a_ref[...], w_ref[...], preferred_element_type=F32)
    o_ref[...] = res_ref[...] + gate_ref[0] * acc


def _mm_res(a, w, res, gate, *, name):
    n, k = a.shape
    m = w.shape[1]
    per_mod = n // gate.shape[0]
    tm = _pick(per_mod, (1024, 512, 256, 128))
    tn = _pick(m, (512, 256, 128))
    tiles_per_mod = per_mod // tm
    return pl.pallas_call(
        _mm_res_kernel,
        grid=(n // tm, m // tn),
        in_specs=[
            pl.BlockSpec((tm, k), lambda i, j: (i, 0)),
            pl.BlockSpec((k, tn), lambda i, j: (0, j)),
            pl.BlockSpec((tm, tn), lambda i, j: (i, j)),
            pl.BlockSpec((1, 1, tn), lambda i, j: (i // tiles_per_mod, 0, j)),
        ],
        out_specs=pl.BlockSpec((tm, tn), lambda i, j: (i, j)),
        out_shape=jax.ShapeDtypeStruct((n, m), F32),
        compiler_params=_params("parallel", "parallel"),
        name=name,
    )(a, w, res, gate)


def _rope(y, cos, sin_up, sin_dn, n_freq):
    w = y.shape[-1]
    return y * cos + pltpu.roll(y, w - n_freq, 1) * sin_up + pltpu.roll(y, n_freq, 1) * sin_dn


def _sumsq(x):
    return jnp.sum(x * x, axis=-1, keepdims=True)


def _prep_aq_kernel(*refs, rope, scale):
    if rope:
        x_ref, gn_ref, gr_ref, cos_ref, su_ref, sd_ref, o_ref = refs
    else:
        x_ref, gn_ref, gr_ref, o_ref = refs
    gn, gr = gn_ref[...], gr_ref[...]
    for h in range(A_HEADS):
        xn = x_ref[:, h * 128:(h + 1) * 128]
        xr = x_ref[:, 1024 + h * 128:1024 + (h + 1) * 128]
        r = lax.rsqrt((_sumsq(xn) + _sumsq(xr)) * (1.0 / A_QK) + EPS)
        yn = xn * r * gn
        yr = xr * r * gr
        if rope:
            yr = _rope(yr, cos_ref[...], su_ref[...], sd_ref[...], A_ROPE // 4)
        o_ref[0, h, 0, :, 0:128] = (yn * scale).astype(BF16)
        o_ref[0, h, 0, :, 128:256] = (yr * scale).astype(BF16)


def _prep_akv_kernel(*refs, rope):
    if rope:
        x_ref, kr_ref, gn_ref, gr_ref, cos_ref, su_ref, sd_ref, k_ref, v_ref = refs
    else:
        x_ref, kr_ref, gn_ref, gr_ref, k_ref, v_ref = refs
    gn = gn_ref[...]
    kr = kr_ref[...]
    ss_r = _sumsq(kr)
    krg = kr * gr_ref[...]
    if rope:
        krg = _rope(krg, cos_ref[...], su_ref[...], sd_ref[...], A_ROPE // 4)
    for h in range(A_HEADS):
        xn = x_ref[:, h * 256:h * 256 + 128]
        r = lax.rsqrt((_sumsq(xn) + ss_r) * (1.0 / A_QK) + EPS)
        k_ref[0, h, :, 0:128] = (xn * r * gn).astype(BF16)
        k_ref[0, h, :, 128:256] = (krg * r).astype(BF16)
        v_ref[0, h] = x_ref[:, h * 256 + 128:(h + 1) * 256].astype(BF16)


def _norm64_pairs(x, g):
    x2 = x * x
    lo = lax.broadcasted_iota(jnp.int32, x.shape, 1) < B_DIM
    s_lo = jnp.sum(jnp.where(lo, x2, 0.0), axis=-1, keepdims=True)
    s_hi = jnp.sum(jnp.where(lo, 0.0, x2), axis=-1, keepdims=True)
    r = jnp.where(lo, lax.rsqrt(s_lo * (1.0 / B_DIM) + EPS), lax.rsqrt(s_hi * (1.0 / B_DIM) + EPS))
    return x * r * g


def _prep_b_kernel(*refs, rope, scale):
    if rope:
        q_ref, k_ref, v_ref, gq_ref, gk_ref, cos_ref, su_ref, sd_ref, qo_ref, ko_ref, vo_ref = refs
    else:
        q_ref, k_ref, v_ref, gq_ref, gk_ref, qo_ref, ko_ref, vo_ref = refs
    for src, g_ref, dst, mul in ((q_ref, gq_ref, qo_ref, scale), (k_ref, gk_ref, ko_ref, None)):
        g = g_ref[...]
        for j in range(B_HEADS):
            y = _norm64_pairs(src[:, j * 128:(j + 1) * 128], g)
            if rope:
                y = _rope(y, cos_ref[...], su_ref[...], sd_ref[...], B_DIM // 4)
            if mul is not None:
                y = y * mul
            yb = y.astype(BF16)
            if dst is qo_ref:
                dst[0, 2 * j, 0] = yb[:, 0:B_DIM]
                dst[0, 2 * j + 1, 0] = yb[:, B_DIM:2 * B_DIM]
            else:
                dst[0, 2 * j] = yb[:, 0:B_DIM]
                dst[0, 2 * j + 1] = yb[:, B_DIM:2 * B_DIM]
    for h in range(B_HEADS):
        vo_ref[0, h] = v_ref[:, h * B_V:(h + 1) * B_V].astype(BF16)


def _prep_c_kernel(*refs, rope, scale):
    if rope:
        q_ref, kv_ref, gq_ref, gk_ref, cos_ref, su_ref, sd_ref, qo_ref, ko_ref, vo_ref = refs
    else:
        q_ref, kv_ref, gq_ref, gk_ref, qo_ref, ko_ref, vo_ref = refs

    def norm_rope(x, g):
        y = x * lax.rsqrt(_sumsq(x) * (1.0 / C_DIM) + EPS) * g
        if rope:
            y = _rope(y, cos_ref[...], su_ref[...], sd_ref[...], C_DIM // 4)
        return y

    gq, gk = gq_ref[...], gk_ref[...]
    for h in range(C_HEADS):
        y = norm_rope(q_ref[:, h * C_DIM:(h + 1) * C_DIM], gq) * scale
        qo_ref[0, h // C_GROUP, h % C_GROUP] = y.astype(BF16)
    for h in range(C_KV_HEADS):
        ko_ref[0, h] = norm_rope(kv_ref[:, h * C_DIM:(h + 1) * C_DIM], gk).astype(BF16)
        vo_ref[0, h] = kv_ref[:, (C_KV_HEADS + h) * C_DIM:(C_KV_HEADS + h + 1) * C_DIM].astype(BF16)


def _prep_call(kernel, ins, in_blocks, consts, tables, outs, *, batch, seq, name):
    ts = _pick(seq, (256, 128))
    n_s = seq // ts
    in_specs = [pl.BlockSpec((ts, w), functools.partial(lambda b, i, c: (b * n_s + i, c), c=c)) for w, c in in_blocks]
    in_specs += [pl.BlockSpec(c.shape, lambda b, i: (0, 0)) for c in consts]
    if tables is not None:
        in_specs += [pl.BlockSpec((ts, LANE_V7X), lambda b, i: (i, 0)) for _ in tables]
    out_specs, out_shape = [], []
    for heads, d in outs:
        nz = len(heads)
        out_specs.append(pl.BlockSpec((1,) + heads + (ts, d),
                                      functools.partial(lambda b, i, nz: (b,) + (0,) * nz + (i, 0), nz=nz)))
        out_shape.append(jax.ShapeDtypeStruct((batch,) + heads + (seq, d), BF16))
    args = list(ins) + list(consts) + (list(tables) if tables is not None else [])
    return pl.pallas_call(
        kernel,
        grid=(batch, n_s),
        in_specs=in_specs,
        out_specs=out_specs,
        out_shape=out_shape,
        compiler_params=_params("parallel", "parallel"),
        name=name,
    )(*args)


def _flash_kernel(*refs, group, tq, n_src, tk):
    q_ref, kv_refs, o_ref = refs[0], refs[1:1 + 2 * n_src], refs[1 + 2 * n_src]
    m_scr, l_scr, acc_scr = refs[2 + 2 * n_src:]
    dv = LANE_V7X
    m_scr[...] = jnp.full(m_scr.shape, NEG_INF, F32)
    l_scr[...] = jnp.zeros(l_scr.shape, F32)
    acc_scr[...] = jnp.zeros(acc_scr.shape, F32)

    q = q_ref[0, 0].reshape(group * tq, q_ref.shape[-1])
    ones = jnp.ones((tk, LANE_V7X), BF16)
    for src in range(n_src):
        k_ref, v_ref = kv_refs[2 * src], kv_refs[2 * src + 1]
        for t in range(k_ref.shape[2] // tk):
            k = k_ref[0, 0, t * tk:(t + 1) * tk, :]
            v1 = jnp.concatenate([v_ref[0, 0, t * tk:(t + 1) * tk, :], ones], axis=1)
            s = lax.dot_general(q, k, (((1,), (1,)), ((), ())), preferred_element_type=F32)
            m_prev = m_scr[...]
            m_new = jnp.maximum(m_prev, jnp.max(s, axis=-1, keepdims=True))
            alpha = jnp.exp2(m_prev - m_new)
            p = jnp.exp2(s - jnp.concatenate([m_new] * (tk // LANE_V7X), axis=1))
            pv = jnp.dot(p.astype(BF16), v1, preferred_element_type=F32)
            acc_scr[...] = alpha * acc_scr[...] + pv[:, :dv]
            l_scr[...] = alpha * l_scr[...] + pv[:, dv:]
            m_scr[...] = m_new

    out = acc_scr[...] / l_scr[...]
    for g in range(group):
        o_ref[0, :, g * dv:(g + 1) * dv] = out[g * tq:(g + 1) * tq].astype(o_ref.dtype)


def _flash(q, kv_sources, *, out_dtype, name):
    b, hk, group, s, d = q.shape
    hv, dv = kv_sources[0][1].shape[1], kv_sources[0][1].shape[3]
    assert dv == LANE_V7X
    k_per_v = hk // hv
    tq = _pick(s, (512 // group, 256 // group, 128 // group))
    tk = 256 if all(k.shape[2] % 256 == 0 for k, _ in kv_sources) else LANE_V7X
    rows = group * tq
    stat = pltpu.VMEM((rows, LANE_V7X), F32)
    in_specs = [pl.BlockSpec((1, 1, group, tq, d), lambda bi, h, i: (bi, h, 0, i, 0))]
    args = [q]
    for k, v in kv_sources:
        t = k.shape[2]
        in_specs += [pl.BlockSpec((1, 1, t, d), lambda bi, h, i: (bi, h, 0, 0)),
                     pl.BlockSpec((1, 1, t, dv), lambda bi, h, i: (bi, h // k_per_v, 0, 0))]
        args += [k, v]
    return pl.pallas_call(
        functools.partial(_flash_kernel, group=group, tq=tq, n_src=len(kv_sources), tk=tk),
        grid=(b, hk, s // tq),
        in_specs=in_specs,
        out_specs=pl.BlockSpec((1, tq, group * dv), lambda bi, h, i: (bi, i, h)),
        out_shape=jax.ShapeDtypeStruct((b, s, hk * group * dv), out_dtype),
        scratch_shapes=[stat, stat, stat],
        compiler_params=_params("parallel", "parallel", "parallel"),
        name=name,
    )(*args)


def _merge_ab_kernel(oa_ref, ob_ref, g_ref, lv_ref, o_ref, *, lam_init):
    lv = lv_ref[...]
    lam = (jnp.exp(jnp.sum(lv[0:1] * lv[1:2], axis=-1, keepdims=True))
           - jnp.exp(jnp.sum(lv[2:3] * lv[3:4], axis=-1, keepdims=True)) + lam_init)
    n_a = A_HEADS * A_V
    o_ref[:, 0:n_a] = oa_ref[...].astype(BF16)
    g = g_ref[...]
    for h in range(B_HEADS):
        o = ob_ref[:, 2 * h * B_V:(2 * h + 1) * B_V] - lam * ob_ref[:, (2 * h + 1) * B_V:(2 * h + 2) * B_V]
        y = o * lax.rsqrt(_sumsq(o) * (1.0 / B_V) + EPS) * g * (1.0 - lam_init)
        o_ref[:, n_a + h * B_V:n_a + (h + 1) * B_V] = y.astype(BF16)


def _merge_ab(oa, ob, g_sub, lam_vec, lam_init):
    n = oa.shape[0]
    tm = _pick(n, (512, 256, 128))
    wa, wb = oa.shape[1], ob.shape[1]
    wo = wa + B_HEADS * B_V
    return pl.pallas_call(
        functools.partial(_merge_ab_kernel, lam_init=lam_init),
        grid=(n // tm,),
        in_specs=[
            pl.BlockSpec((tm, wa), lambda i: (i, 0)),
            pl.BlockSpec((tm, wb), lambda i: (i, 0)),
            pl.BlockSpec((1, B_V), lambda i: (0, 0)),
            pl.BlockSpec((4, B_DIM), lambda i: (0, 0)),
        ],
        out_specs=pl.BlockSpec((tm, wo), lambda i: (i, 0)),
        out_shape=jax.ShapeDtypeStruct((n, wo), BF16),
        compiler_params=_params("parallel"),
        name="merge_ab",
    )(oa, ob, g_sub.reshape(1, B_V), lam_vec)


def _top_rows(s, k):
    vals = []
    cur = s
    for _ in range(k):
        mx = jnp.max(cur, axis=0, keepdims=True)
        vals.append(mx)
        cur = jnp.where(cur == mx, NEG_INF, cur)
    return jnp.concatenate(vals, axis=0)


def _peer_stats_kernel(q_ref, keys_ref, c1_ref, e2_ref, th_ref):
    dn = (((1,), (1,)), ((), ()))
    s1 = lax.dot_general(keys_ref[0], q_ref[:, 0:N_KEYS].astype(BF16), dn, preferred_element_type=F32)
    s2 = lax.dot_general(keys_ref[1], q_ref[:, N_KEYS:2 * N_KEYS].astype(BF16), dn, preferred_element_type=F32)
    v1 = _top_rows(s1, P_TOPK)
    v2 = _top_rows(s2, P_TOPK)
    n_b = [P_TOPK // (a + 1) for a in range(P_TOPK)]
    cand = jnp.concatenate([v1[a:a + 1] + v2[0:n_b[a]] for a in range(P_TOPK)], axis=0)
    best = _top_rows(cand, P_TOPK)
    inv_z = 1.0 / jnp.sum(jnp.exp(best - best[0:1]), axis=0, keepdims=True)
    c1_ref[0] = jnp.exp(s1 - v1[0:1]) * inv_z
    e2_ref[0] = jnp.exp(s2 - v2[0:1])
    c1v = jnp.exp(v1 - v1[0:1]) * inv_z
    e2v = jnp.exp(v2 - v2[0:1])
    gates = jnp.concatenate([c1v[a:a + 1] * e2v[0:n_b[a]] for a in range(P_TOPK)], axis=0)
    th_ref[0] = jnp.min(jnp.where(cand >= best[P_TOPK - 1:P_TOPK], gates, jnp.inf), axis=0, keepdims=True)


def _peer_stats(q, keys):
    n = q.shape[0]
    tm = _pick(n, (256, 128))
    big = pl.BlockSpec((1, N_KEYS, tm), lambda i, h: (h, 0, i))
    big_shape = jax.ShapeDtypeStruct((P_HEADS, N_KEYS, n), F32)
    return pl.pallas_call(
        _peer_stats_kernel,
        grid=(n // tm, P_HEADS),
        in_specs=[
            pl.BlockSpec((tm, P_DQ), lambda i, h: (i, h)),
            pl.BlockSpec((2, N_KEYS, P_DQ // 2), lambda i, h: (h, 0, 0)),
        ],
        out_specs=[big, big, pl.BlockSpec((1, 1, tm), lambda i, h: (h, 0, i))],
        out_shape=[big_shape, big_shape, jax.ShapeDtypeStruct((P_HEADS, 1, n), F32)],
        compiler_params=_params("parallel", "parallel"),
        name="peer_stats",
    )(q, keys)


def _gelu_tanh(a):
    return 0.5 * a * (1.0 + jnp.tanh(math.sqrt(2.0 / math.pi) * (a + 0.044715 * (a * a * a))))


def _peer_dense_kernel(h_ref, u_ref, vt_ref, c1_ref, e2_ref, th_ref, res_ref, gate_ref,
                       o_ref, acc_ref, wt_ref, *, tch, n_ch, tm, tc, kt):
    j = pl.program_id(1)

    @pl.when(j == 0)
    def _():
        acc_ref[...] = jnp.zeros(acc_ref.shape, F32)

    n_r = tch // N_KEYS
    n_kp = h_ref.shape[1] // kt
    units = [(ch, c) for ch in range(n_ch) for c in range(tm // tc)]

    def u_piece(k, kk, dep):
        ch, c = units[k]
        hs = h_ref[c * tc:(c + 1) * tc, kk * kt:(kk + 1) * kt]
        if dep is not None:
            z16 = jnp.concatenate([dep, dep], axis=0).astype(BF16)
            hs = hs + jnp.concatenate([jnp.concatenate([z16] * (kt // LANE_V7X), axis=1)] * (tc // 16), axis=0)
        return lax.dot_general(u_ref[ch * tch:(ch + 1) * tch, kk * kt:(kk + 1) * kt], hs,
                               (((1,), (1,)), ((), ())), preferred_element_type=F32)

    def block_zero(m):
        t = m[0:8]
        for i in range(1, m.shape[0] // 8):
            t = t + m[8 * i:8 * i + 8]
        s = t[:, 0:LANE_V7X]
        for i in range(1, m.shape[1] // LANE_V7X):
            s = s + t[:, i * LANE_V7X:(i + 1) * LANE_V7X]
        return jnp.minimum(s, 0.0)

    at = None
    for kk in range(n_kp):
        p = u_piece(0, kk, None)
        at = p if at is None else at + p
    dep = None
    owner = [kk * max(n_r - 1, 1) // n_kp for kk in range(n_kp)]
    for k, (ch, c) in enumerate(units):
        cols = slice(c * tc, (c + 1) * tc)
        at_next = None
        for r in range(n_r):
            if k + 1 < len(units):
                for kk in [i for i in range(n_kp) if owner[i] == r]:
                    p = u_piece(k + 1, kk, dep)
                    at_next = p if at_next is None else at_next + p
            i1 = ch * n_r + r
            m = jnp.zeros((N_KEYS, tc), F32)
            for h in range(P_HEADS):
                g = e2_ref[h, :, cols] * c1_ref[h, i1:i1 + 1, cols]
                m = m + jnp.where(g >= th_ref[h, :, cols], g, 0.0)
            dep = block_zero(m)
            wt_ref[ch * tch + r * N_KEYS:ch * tch + (r + 1) * N_KEYS, cols] = (
                m * _gelu_tanh(at[r * N_KEYS:(r + 1) * N_KEYS, :])).astype(BF16)
        at = at_next
    acc_ref[...] += jnp.dot(vt_ref[...], wt_ref[...], preferred_element_type=F32)

    @pl.when(j == pl.num_programs(1) - 1)
    def _():
        o_ref[...] = res_ref[...] + gate_ref[0] * acc_ref[...].T


def _peer_dense(h, u, vt, layer, stats, res, gate):
    n, d = h.shape
    e = u.shape[1]
    c1, e2, theta = stats
    per_mod = n // gate.shape[0]
    tm = _pick(per_mod, (512, 256, 128))
    te = _pick(e, (1024,))
    tch = _pick(te, (512,))
    tc = _pick(tm, (256, 128))
    kt = _pick(d, (256,))
    tiles_per_mod = per_mod // tm
    once = dict(pipeline_mode=pl.Buffered(1))
    big = pl.BlockSpec((P_HEADS, N_KEYS, tm), lambda i, j: (0, 0, i), **once)
    row = pl.BlockSpec((P_HEADS, te // N_KEYS, tm), lambda i, j: (0, j, i))
    return pl.pallas_call(
        functools.partial(_peer_dense_kernel, tch=tch, n_ch=te // tch, tm=tm, tc=tc, kt=kt),
        grid=(n // tm, e // te),
        in_specs=[
            pl.BlockSpec((tm, d), lambda i, j: (i, 0)),
            pl.BlockSpec((None, te, d), lambda i, j: (layer, j, 0)),
            pl.BlockSpec((None, d, te), lambda i, j: (layer, 0, j)),
            row, big,
            pl.BlockSpec((P_HEADS, 1, tm), lambda i, j: (0, 0, i)),
            pl.BlockSpec((tm, d), lambda i, j: (i, 0), **once),
            pl.BlockSpec((1, 1, d), lambda i, j: (i // tiles_per_mod, 0, 0)),
        ],
        out_specs=pl.BlockSpec((tm, d), lambda i, j: (i, 0), **once),
        out_shape=jax.ShapeDtypeStruct((n, d), F32),
        scratch_shapes=[pltpu.VMEM((d, tm), F32), pltpu.VMEM((te, tm), BF16)],
        compiler_params=_params("parallel", "arbitrary"),
        name="peer_dense",
    )(h, u, vt, c1, e2, theta, res, gate)


def _rope_tables(seq, rot_dim):
    n_freq = rot_dim // 4
    freqs = ROPE_BASE ** (-jnp.arange(n_freq, dtype=F32) / n_freq)
    pos = jnp.arange(seq)
    row = (pos // GRID_W).astype(F32)[:, None] * freqs
    col = (pos % GRID_W).astype(F32)[:, None] * freqs
    zero = jnp.zeros_like(row)
    cos = jnp.concatenate([jnp.cos(row), jnp.cos(row), jnp.cos(col), jnp.cos(col)], axis=-1)
    s_up = jnp.concatenate([-jnp.sin(row), zero, -jnp.sin(col), zero], axis=-1)
    s_dn = jnp.concatenate([zero, jnp.sin(row), zero, jnp.sin(col)], axis=-1)
    return cos, s_up, s_dn


def _tile_tables(tabs, reps, width):
    out = []
    for idx, t in enumerate(tabs):
        t = jnp.tile(t, (1, reps))
        pad = width - t.shape[1]
        if pad:
            fill = jnp.ones if idx == 0 else jnp.zeros
            t = jnp.concatenate([t, fill((t.shape[0], pad), F32)], axis=-1)
        out.append(t)
    return out


def _pad_lanes(v, width):
    return jnp.pad(v, (0, width - v.shape[0])).reshape(1, width)


def _layout_w_in_ab(w):
    d = w.shape[0]
    cq, ckv, kr, bq, bk, bv = jnp.split(w, [768, 1280, 1344, 2368, 3392], axis=1)
    return jnp.concatenate([cq, kr, jnp.zeros((d, AB_BQ - AB_KR - A_ROPE), w.dtype), bq, bk, bv, ckv],
                           axis=1).astype(BF16)


def _layout_w_uq(w):
    k = w.shape[0]
    w3 = w.reshape(k, A_HEADS, A_QK)
    nope = w3[:, :, :A_NOPE].reshape(k, A_HEADS * A_NOPE)
    rope = jnp.pad(w3[:, :, A_NOPE:], ((0, 0), (0, 0), (0, LANE_V7X - A_ROPE))).reshape(k, A_HEADS * LANE_V7X)
    return jnp.concatenate([nope, rope], axis=1).astype(BF16)


def _mixer_ab(tok, mod, p, tables, *, batch, seq, rope):
    proj = _nmm(tok, p["g_norm1"], p["w_in"], shift=mod[0], scale=mod[1], name="ab_in")
    qa_raw = _nmm(proj, p["g_cq"], p["w_uq"], k_block=AB_CQ // A_Q_LORA, name="ab_uq")
    kva_raw = _nmm(proj, p["g_ckv"], p["w_ukv"], k_block=AB_CKV // A_KV_LORA, name="ab_ukv")
    sa, sb = A_QK ** -0.5 * LOG2_E, B_DIM ** -0.5 * LOG2_E
    (qa,) = _prep_call(
        functools.partial(_prep_aq_kernel, rope=rope, scale=sa), [qa_raw], [(qa_raw.shape[1], 0)],
        [p["gqa_n"], p["gqa_r"]], tables["a"] if rope else None, [((A_HEADS, 1), 2 * LANE_V7X)],
        batch=batch, seq=seq, name="prep_aq")
    ka, va = _prep_call(
        functools.partial(_prep_akv_kernel, rope=rope), [kva_raw, proj],
        [(kva_raw.shape[1], 0), (LANE_V7X, AB_KR // LANE_V7X)],
        [p["gka_n"], p["gka_r"]], tables["a"] if rope else None,
        [((A_HEADS,), 2 * LANE_V7X), ((A_HEADS,), A_V)], batch=batch, seq=seq, name="prep_akv")
    wb = B_HEADS * 2 * B_DIM
    qb, kb, vb = _prep_call(
        functools.partial(_prep_b_kernel, rope=rope, scale=sb), [proj, proj, proj],
        [(wb, AB_BQ // wb), (wb, AB_BK // wb), (wb, AB_BV // wb)],
        [p["gqb"], p["gkb"]], tables["b"] if rope else None,
        [((2 * B_HEADS, 1), B_DIM), ((2 * B_HEADS,), B_DIM), ((B_HEADS,), B_V)],
        batch=batch, seq=seq, name="prep_b")
    return (qa, ka, va), (qb, kb, vb)


def _mixer_c(tok, mod, p, tables, *, batch, seq, rope):
    proj = _nmm(tok, p["g_norm1"], p["w_in"], shift=mod[0], scale=mod[1], name="c_in")
    wq = C_HEADS * C_DIM
    wkv = 2 * C_KV_HEADS * C_DIM
    return _prep_call(
        functools.partial(_prep_c_kernel, rope=rope, scale=C_DIM ** -0.5 * LOG2_E), [proj, proj],
        [(wq, 0), (wkv, wq // wkv)], [p["gq"], p["gk"]], tables["c"] if rope else None,
        [((C_KV_HEADS, C_GROUP), C_DIM), ((C_KV_HEADS,), C_DIM), ((C_KV_HEADS,), C_DIM)],
        batch=batch, seq=seq, name="prep_c")


def _peer(tok, mod, gate, p, name):
    q, h = _nmm(tok, p["g_norm2"], p["w_pq"], shift=mod[0], scale=mod[1], emit_h=True, name=name)
    stats = _peer_stats(q, p["keys"])
    return _peer_dense(h, p["u_all"], p["vt_all"], p["layer"], stats, tok, gate)


def kernel(x, c, ctx, c_ctx, w_mod, b_mod, g_norm1, g_norm2, w_in_ab, g_cq, w_uq, g_ckv, w_ukv, g_qn_a, g_kn_a,
           lam_vec, g_qn_b, g_kn_b, g_sub_b, w_out_ab, w_in_c, g_qn_c, g_kn_c, w_out_c, w_pq, sub_keys,
           expert_u, expert_v):
    batch, seq, d = x.shape
    ctx_len = ctx.shape[1]
    depth = w_mod.shape[0]
    assert batch + 1 <= 8

    cz = jnp.concatenate([c, c_ctx[None, :], jnp.zeros((8 - batch - 1, d), F32)], axis=0)
    mod_all = _modulation(cz, w_mod, b_mod).reshape(depth, 8, 6, d)

    tab64 = _rope_tables(seq, A_ROPE)
    tables = {
        "a": _tile_tables(tab64, 1, LANE_V7X),
        "b": _tile_tables(tab64, 2, LANE_V7X),
        "c": _rope_tables(seq, C_DIM),
    }

    xt = x.reshape(batch * seq, d)
    zt = ctx.reshape(batch * ctx_len, d)
    u_all = expert_u.astype(BF16)
    vt_all = jnp.swapaxes(expert_v, 1, 2).astype(BF16)

    for layer in range(depth):
        last = layer == depth - 1
        e = layer // 2
        mx = [mod_all[layer, :batch, i][:, None, :] for i in range(6)]
        mz = [mod_all[layer, batch:batch + 1, i][:, None, :] for i in range(6)]
        pp = {
            "g_norm2": g_norm2[layer], "w_pq": w_pq[layer].astype(BF16),
            "keys": sub_keys[layer].reshape(2 * P_HEADS, N_KEYS, P_DQ // 2).astype(BF16),
            "u_all": u_all, "vt_all": vt_all, "layer": layer,
        }
        if layer % 2 == 0:
            lam_init = 0.8 - 0.6 * math.exp(-0.3 * layer)
            p = {
                "g_norm1": g_norm1[layer], "w_in": _layout_w_in_ab(w_in_ab[e]),
                "g_cq": g_cq[e], "w_uq": _layout_w_uq(w_uq[e]),
                "g_ckv": g_ckv[e], "w_ukv": w_ukv[e].astype(BF16),
                "gqa_n": g_qn_a[e][:A_NOPE].reshape(1, A_NOPE), "gqa_r": _pad_lanes(g_qn_a[e][A_NOPE:], LANE_V7X),
                "gka_n": g_kn_a[e][:A_NOPE].reshape(1, A_NOPE), "gka_r": _pad_lanes(g_kn_a[e][A_NOPE:], LANE_V7X),
                "gqb": jnp.tile(g_qn_b[e], 2).reshape(1, 2 * B_DIM), "gkb": jnp.tile(g_kn_b[e], 2).reshape(1, 2 * B_DIM),
            }
            w_out = w_out_ab[e].astype(BF16)
            (qa, ka, va), (qb, kb, vb) = _mixer_ab(xt, mx, p, tables, batch=batch, seq=seq, rope=True)
            (qaz, kaz, vaz), (qbz, kbz, vbz) = _mixer_ab(zt, mz, p, tables, batch=batch, seq=ctx_len, rope=False)
            oa = _flash(qa, [(ka, va), (kaz, vaz)], out_dtype=BF16, name="flash_a")
            ob = _flash(qb, [(kb, vb), (kbz, vbz)], out_dtype=F32, name="flash_b")
            att = _merge_ab(oa.reshape(batch * seq, -1), ob.reshape(batch * seq, -1), g_sub_b[e], lam_vec[e], lam_init)
            if not last:
                oaz = _flash(qaz, [(kaz, vaz)], out_dtype=BF16, name="flash_az")
                obz = _flash(qbz, [(kbz, vbz)], out_dtype=F32, name="flash_bz")
                attz = _merge_ab(oaz.reshape(batch * ctx_len, -1), obz.reshape(batch * ctx_len, -1),
                                 g_sub_b[e], lam_vec[e], lam_init)
        else:
            p = {
                "g_norm1": g_norm1[layer], "w_in": w_in_c[e].astype(BF16),
                "gq": g_qn_c[e].reshape(1, C_DIM), "gk": g_kn_c[e].reshape(1, C_DIM),
            }
            w_out = w_out_c[e].astype(BF16)
            q, k, v = _mixer_c(xt, mx, p, tables, batch=batch, seq=seq, rope=True)
            qz, kz, vz = _mixer_c(zt, mz, p, tables, batch=batch, seq=ctx_len, rope=False)
            att = _flash(q, [(k, v), (kz, vz)], out_dtype=BF16, name="flash_c").reshape(batch * seq, -1)
            if not last:
                attz = _flash(qz, [(kz, vz)], out_dtype=BF16, name="flash_cz").reshape(batch * ctx_len, -1)

        xt = _mm_res(att, w_out, xt, mx[2], name="out_proj")
        xt = _peer(xt, (mx[3], mx[4]), mx[5], pp, "peer_q")
        if not last:
            zt = _mm_res(attz, w_out, zt, mz[2], name="out_proj_z")
            zt = _peer(zt, (mz[3], mz[4]), mz[5], pp, "peer_qz")
    return xt.reshape(batch, seq, d)
```

```python
import functools
import math

import jax
import jax.numpy as jnp
from jax import lax
from jax.experimental import pallas as pl
from jax.experimental.pallas import tpu as pltpu

F32 = jnp.float32
BF16 = jnp.bfloat16

LANE_V7X = 128
VMEM_LIMIT_V7X = 56 * 1024 * 1024

GRID_W = 64
ROPE_BASE = 10000.0
EPS = 1e-6
A_HEADS, A_Q_LORA, A_KV_LORA, A_NOPE, A_ROPE, A_V = 8, 768, 512, 128, 64, 128
A_QK = A_NOPE + A_ROPE
B_HEADS, B_DIM, B_V = 8, 64, 128
C_HEADS, C_KV_HEADS, C_DIM = 16, 4, 128
C_GROUP = C_HEADS // C_KV_HEADS
P_HEADS, N_KEYS, P_DQ, P_TOPK = 8, 128, 256, 16
N_EXPERTS = N_KEYS * N_KEYS
PEER_TE = 1024
FLASH_TK = 256
FLASH_ROWS = 2048
NEG_INF = float("-inf")
LOG2_E = math.log2(math.e)

AB_CQ, AB_KR, AB_BQ, AB_BK, AB_BV, AB_CKV, AB_COLS = 0, 768, 1024, 2048, 3072, 4096, 4608


def _pick(n, prefs):
    for p in prefs:
        if n % p == 0:
            return p
    return n


def _col_tiles(w, tn):
    k, m = w.shape
    return w.reshape(k, m // tn, tn).transpose(1, 0, 2)


def _params(*sem):
    return pltpu.CompilerParams(dimension_semantics=sem, vmem_limit_bytes=VMEM_LIMIT_V7X)


def _mod_kernel(c_ref, w_ref, b_ref, o_ref):
    c = c_ref[...]
    s = c * (1.0 / (1.0 + jnp.exp(-c)))
    o_ref[0] = jnp.dot(s.astype(BF16), w_ref[0].astype(BF16), preferred_element_type=F32) + b_ref[0]


def _modulation(cz, w_mod, b_mod):
    depth, d, n = w_mod.shape
    tn = _pick(n, (1024, 768, 512, 256, 128))
    return pl.pallas_call(
        _mod_kernel,
        grid=(depth, n // tn),
        in_specs=[
            pl.BlockSpec((8, d), lambda l, j: (0, 0)),
            pl.BlockSpec((1, d, tn), lambda l, j: (l, 0, j)),
            pl.BlockSpec((1, 1, tn), lambda l, j: (l, 0, j)),
        ],
        out_specs=pl.BlockSpec((1, 8, tn), lambda l, j: (l, 0, j)),
        out_shape=jax.ShapeDtypeStruct((depth, 8, n), F32),
        compiler_params=_params("parallel", "parallel"),
        name="modulation",
    )(cz, w_mod, b_mod.reshape(depth, 1, n))


def _nmm_kernel(*refs, has_mod, emit_h):
    if has_mod:
        x_ref, g_ref, sh_ref, sc_ref, w_ref = refs[:5]
        rest = refs[5:]
    else:
        x_ref, g_ref, w_ref = refs[:3]
        rest = refs[3:]
    if emit_h:
        o_ref, ho_ref, h_scr = rest
    else:
        o_ref, h_scr = rest

    @pl.when(pl.program_id(1) == 0)
    def _():
        x = x_ref[...].astype(F32)
        y = x * lax.rsqrt(jnp.mean(x * x, axis=-1, keepdims=True) + EPS) * g_ref[...]
        if has_mod:
            y = y * (1.0 + sc_ref[0]) + sh_ref[0]
        h_scr[...] = y.astype(BF16)
        if emit_h:
            ho_ref[...] = y.T.astype(BF16)

    o_ref[...] = jnp.dot(h_scr[...], w_ref[...], preferred_element_type=F32).astype(o_ref.dtype)


def _nmm(x, g, w, *, k_block=0, shift=None, scale=None, emit_h=False, out_dtype=F32, name):
    n = x.shape[0]
    k, m = w.shape
    has_mod = shift is not None
    n_mod = shift.shape[0] if has_mod else 1
    per_mod = n // n_mod
    tm = _pick(per_mod, (1024, 512, 256, 128))
    tn = _pick(m, (512, 768, 256, 128))
    tiles_per_mod = per_mod // tm
    in_specs = [pl.BlockSpec((tm, k), lambda i, j: (i, k_block)), pl.BlockSpec((1, k), lambda i, j: (0, 0))]
    args = [x, g.reshape(1, k)]
    if has_mod:
        mod_spec = pl.BlockSpec((1, 1, k), lambda i, j: (i // tiles_per_mod, 0, 0))
        in_specs += [mod_spec, mod_spec]
        args += [shift, scale]
    in_specs.append(pl.BlockSpec((None, k, tn), lambda i, j: (j, 0, 0)))
    args.append(_col_tiles(w, tn))
    out_specs = pl.BlockSpec((tm, tn), lambda i, j: (i, j))
    out_shape = jax.ShapeDtypeStruct((n, m), out_dtype)
    if emit_h:
        out_specs = [out_specs, pl.BlockSpec((k, tm), lambda i, j: (0, i))]
        out_shape = [out_shape, jax.ShapeDtypeStruct((k, n), BF16)]
    return pl.pallas_call(
        functools.partial(_nmm_kernel, has_mod=has_mod, emit_h=emit_h),
        grid=(n // tm, m // tn),
        in_specs=in_specs,
        out_specs=out_specs,
        out_shape=out_shape,
        scratch_shapes=[pltpu.VMEM((tm, k), BF16)],
        compiler_params=_params("parallel", "arbitrary"),
        name=name,
    )(*args)


def _mm_res_kernel(a_ref, w_ref, res_ref, gate_ref, o_ref):
    acc = jnp.dot(a_ref[...], w_ref[...], preferred_element_type=F32)
    o_ref[...] = res_ref[...] + gate_ref[0] * acc


def _mm_res(a, w, res, gate, *, name):
    n, k = a.shape
    m = w.shape[1]
    per_mod = n // gate.shape[0]
    tm = _pick(per_mod, (1024, 512, 256, 128))
    tn = _pick(m, (512, 256, 128))
    tiles_per_mod = per_mod // tm
    return pl.pallas_call(
        _mm_res_kernel,
        grid=(n // tm, m // tn),
        in_specs=[
            pl.BlockSpec((tm, k), lambda i, j: (i, 0)),
            pl.BlockSpec((None, k, tn), lambda i, j: (j, 0, 0)),
            pl.BlockSpec((tm, tn), lambda i, j: (i, j)),
            pl.BlockSpec((1, 1, tn), lambda i, j: (i // tiles_per_mod, 0, j)),
        ],
        out_specs=pl.BlockSpec((tm, tn), lambda i, j: (i, j)),
        out_shape=jax.ShapeDtypeStruct((n, m), F32),
        compiler_params=_params("parallel", "parallel"),
        name=name,
    )(a, _col_tiles(w, tn), res, gate)


def _rope(y, cos, sin_up, sin_dn, n_freq):
    w = y.shape[-1]
    return y * cos + pltpu.roll(y, w - n_freq, 1) * sin_up + pltpu.roll(y, n_freq, 1) * sin_dn


def _sumsq(x):
    return jnp.sum(x * x, axis=-1, keepdims=True)


def _prep_aq_kernel(*refs, rope, scale):
    if rope:
        x_ref, gn_ref, gr_ref, cos_ref, su_ref, sd_ref, o_ref = refs
    else:
        x_ref, gn_ref, gr_ref, o_ref = refs
    gn, gr = gn_ref[...], gr_ref[...]
    for h in range(A_HEADS):
        xn = x_ref[:, h * 128:(h + 1) * 128]
        xr = x_ref[:, 1024 + h * 128:1024 + (h + 1) * 128]
        r = lax.rsqrt((_sumsq(xn) + _sumsq(xr)) * (1.0 / A_QK) + EPS)
        yn = xn * r * gn
        yr = xr * r * gr
        if rope:
            yr = _rope(yr, cos_ref[...], su_ref[...], sd_ref[...], A_ROPE // 4)
        o_ref[0, h, 0, :, 0:128] = (yn * scale).astype(BF16)
        o_ref[0, h, 0, :, 128:256] = (yr * scale).astype(BF16)


def _prep_akv_kernel(*refs, rope):
    if rope:
        x_ref, kr_ref, gn_ref, gr_ref, cos_ref, su_ref, sd_ref, k_ref, v_ref = refs
    else:
        x_ref, kr_ref, gn_ref, gr_ref, k_ref, v_ref = refs
    gn = gn_ref[...]
    kr = kr_ref[...]
    ss_r = _sumsq(kr)
    krg = kr * gr_ref[...]
    if rope:
        krg = _rope(krg, cos_ref[...], su_ref[...], sd_ref[...], A_ROPE // 4)
    for h in range(A_HEADS):
        xn = x_ref[:, h * 256:h * 256 + 128]
        r = lax.rsqrt((_sumsq(xn) + ss_r) * (1.0 / A_QK) + EPS)
        k_ref[0, h, :, 0:128] = (xn * r * gn).astype(BF16)
        k_ref[0, h, :, 128:256] = (krg * r).astype(BF16)
        v_ref[0, h] = x_ref[:, h * 256 + 128:(h + 1) * 256].astype(BF16)


def _norm64_pairs(x, g):
    x2 = x * x
    lo = lax.broadcasted_iota(jnp.int32, x.shape, 1) < B_DIM
    s_lo = jnp.sum(jnp.where(lo, x2, 0.0), axis=-1, keepdims=True)
    s_hi = jnp.sum(jnp.where(lo, 0.0, x2), axis=-1, keepdims=True)
    r = jnp.where(lo, lax.rsqrt(s_lo * (1.0 / B_DIM) + EPS), lax.rsqrt(s_hi * (1.0 / B_DIM) + EPS))
    return x * r * g


def _prep_b_kernel(*refs, rope, scale):
    if rope:
        q_ref, k_ref, v_ref, gq_ref, gk_ref, cos_ref, su_ref, sd_ref, qo_ref, ko_ref, vo_ref = refs
    else:
        q_ref, k_ref, v_ref, gq_ref, gk_ref, qo_ref, ko_ref, vo_ref = refs
    for src, g_ref, dst, mul in ((q_ref, gq_ref, qo_ref, scale), (k_ref, gk_ref, ko_ref, None)):
        g = g_ref[...]
        for j in range(B_HEADS):
            y = _norm64_pairs(src[:, j * 128:(j + 1) * 128], g)
            if rope:
                y = _rope(y, cos_ref[...], su_ref[...], sd_ref[...], B_DIM // 4)
            if mul is not None:
                y = y * mul
            yb = y.astype(BF16)
            if dst is qo_ref:
                dst[0, 2 * j, 0] = yb[:, 0:B_DIM]
                dst[0, 2 * j + 1, 0] = yb[:, B_DIM:2 * B_DIM]
            else:
                dst[0, 2 * j] = yb[:, 0:B_DIM]
                dst[0, 2 * j + 1] = yb[:, B_DIM:2 * B_DIM]
    for h in range(B_HEADS):
        vo_ref[0, h] = v_ref[:, h * B_V:(h + 1) * B_V].astype(BF16)


def _prep_c_kernel(*refs, rope, scale):
    if rope:
        q_ref, kv_ref, gq_ref, gk_ref, cos_ref, su_ref, sd_ref, qo_ref, ko_ref, vo_ref = refs
    else:
        q_ref, kv_ref, gq_ref, gk_ref, qo_ref, ko_ref, vo_ref = refs

    def norm_rope(x, g):
        y = x * lax.rsqrt(_sumsq(x) * (1.0 / C_DIM) + EPS) * g
        if rope:
            y = _rope(y, cos_ref[...], su_ref[...], sd_ref[...], C_DIM // 4)
        return y

    gq, gk = gq_ref[...], gk_ref[...]
    for h in range(C_HEADS):
        y = norm_rope(q_ref[:, h * C_DIM:(h + 1) * C_DIM], gq) * scale
        qo_ref[0, h // C_GROUP, h % C_GROUP] = y.astype(BF16)
    for h in range(C_KV_HEADS):
        ko_ref[0, h] = norm_rope(kv_ref[:, h * C_DIM:(h + 1) * C_DIM], gk).astype(BF16)
        vo_ref[0, h] = kv_ref[:, (C_KV_HEADS + h) * C_DIM:(C_KV_HEADS + h + 1) * C_DIM].astype(BF16)


def _prep_call(kernel, ins, in_blocks, consts, tables, outs, *, batch, seq, name):
    ts = _pick(seq, (256, 128))
    n_s = seq // ts
    in_specs = [pl.BlockSpec((ts, w), functools.partial(lambda b, i, c: (b * n_s + i, c), c=c)) for w, c in in_blocks]
    in_specs += [pl.BlockSpec(c.shape, lambda b, i: (0, 0)) for c in consts]
    if tables is not None:
        in_specs += [pl.BlockSpec((ts, LANE_V7X), lambda b, i: (i, 0)) for _ in tables]
    out_specs, out_shape = [], []
    for heads, d in outs:
        nz = len(heads)
        out_specs.append(pl.BlockSpec((1,) + heads + (ts, d),
                                      functools.partial(lambda b, i, nz: (b,) + (0,) * nz + (i, 0), nz=nz)))
        out_shape.append(jax.ShapeDtypeStruct((batch,) + heads + (seq, d), BF16))
    args = list(ins) + list(consts) + (list(tables) if tables is not None else [])
    return pl.pallas_call(
        kernel,
        grid=(batch, n_s),
        in_specs=in_specs,
        out_specs=out_specs,
        out_shape=out_shape,
        compiler_params=_params("parallel", "parallel"),
        name=name,
    )(*args)


def _flash_kernel(*refs, group, tq, n_src, tks):
    q_ref, kv_refs, o_ref = refs[0], refs[1:1 + 2 * n_src], refs[1 + 2 * n_src]
    m_scr, l_scr, acc_scr = refs[2 + 2 * n_src:]
    dv = LANE_V7X
    m_scr[...] = jnp.full(m_scr.shape, NEG_INF, F32)
    l_scr[...] = jnp.zeros(l_scr.shape, F32)
    acc_scr[...] = jnp.zeros(acc_scr.shape, F32)

    q = q_ref[0, 0].reshape(group * tq, q_ref.shape[-1])
    for src in range(n_src):
        k_ref, v_ref = kv_refs[2 * src], kv_refs[2 * src + 1]
        tk = tks[src]
        ones = jnp.ones((tk, LANE_V7X), BF16)
        for t in range(k_ref.shape[2] // tk):
            k = k_ref[0, 0, t * tk:(t + 1) * tk, :]
            v1 = jnp.concatenate([v_ref[0, 0, t * tk:(t + 1) * tk, :], ones], axis=1)
            s = lax.dot_general(q, k, (((1,), (1,)), ((), ())), preferred_element_type=F32)
            m_prev = m_scr[...]
            m_new = jnp.maximum(m_prev, jnp.max(s, axis=-1, keepdims=True))
            alpha = jnp.exp2(m_prev - m_new)
            p = jnp.exp2(s - jnp.concatenate([m_new] * (tk // LANE_V7X), axis=1))
            pv = jnp.dot(p.astype(BF16), v1, preferred_element_type=F32)
            acc_scr[...] = alpha * acc_scr[...] + pv[:, :dv]
            l_scr[...] = alpha * l_scr[...] + pv[:, dv:]
            m_scr[...] = m_new

    out = acc_scr[...] / l_scr[...]
    for g in range(group):
        o_ref[0, :, g * dv:(g + 1) * dv] = out[g * tq:(g + 1) * tq].astype(o_ref.dtype)


def _flash(q, kv_sources, *, out_dtype, name):
    b, hk, group, s, d = q.shape
    hv, dv = kv_sources[0][1].shape[1], kv_sources[0][1].shape[3]
    assert dv == LANE_V7X
    k_per_v = hk // hv
    tq = _pick(s, (FLASH_ROWS // group, 256 // group, 128 // group))
    tks = tuple(_pick(k.shape[2], (FLASH_TK, 256, LANE_V7X)) for k, _ in kv_sources)
    rows = group * tq
    stat = pltpu.VMEM((rows, LANE_V7X), F32)
    in_specs = [pl.BlockSpec((1, 1, group, tq, d), lambda bi, h, i: (bi, h, 0, i, 0))]
    args = [q]
    for k, v in kv_sources:
        t = k.shape[2]
        in_specs += [pl.BlockSpec((1, 1, t, d), lambda bi, h, i: (bi, h, 0, 0)),
                     pl.BlockSpec((1, 1, t, dv), lambda bi, h, i: (bi, h // k_per_v, 0, 0))]
        args += [k, v]
    return pl.pallas_call(
        functools.partial(_flash_kernel, group=group, tq=tq, n_src=len(kv_sources), tks=tks),
        grid=(b, hk, s // tq),
        in_specs=in_specs,
        out_specs=pl.BlockSpec((1, tq, group * dv), lambda bi, h, i: (bi, i, h)),
        out_shape=jax.ShapeDtypeStruct((b, s, hk * group * dv), out_dtype),
        scratch_shapes=[stat, stat, stat],
        compiler_params=_params("parallel", "parallel", "parallel"),
        name=name,
    )(*args)


def _merge_ab_kernel(oa_ref, ob_ref, g_ref, lv_ref, o_ref, *, lam_init):
    lv = lv_ref[...]
    lam = (jnp.exp(jnp.sum(lv[0:1] * lv[1:2], axis=-1, keepdims=True))
           - jnp.exp(jnp.sum(lv[2:3] * lv[3:4], axis=-1, keepdims=True)) + lam_init)
    n_a = A_HEADS * A_V
    o_ref[:, 0:n_a] = oa_ref[...].astype(BF16)
    g = g_ref[...]
    for h in range(B_HEADS):
        o = ob_ref[:, 2 * h * B_V:(2 * h + 1) * B_V] - lam * ob_ref[:, (2 * h + 1) * B_V:(2 * h + 2) * B_V]
        y = o * lax.rsqrt(_sumsq(o) * (1.0 / B_V) + EPS) * g * (1.0 - lam_init)
        o_ref[:, n_a + h * B_V:n_a + (h + 1) * B_V] = y.astype(BF16)


def _merge_ab(oa, ob, g_sub, lam_vec, lam_init):
    n = oa.shape[0]
    tm = _pick(n, (512, 256, 128))
    wa, wb = oa.shape[1], ob.shape[1]
    wo = wa + B_HEADS * B_V
    return pl.pallas_call(
        functools.partial(_merge_ab_kernel, lam_init=lam_init),
        grid=(n // tm,),
        in_specs=[
            pl.BlockSpec((tm, wa), lambda i: (i, 0)),
            pl.BlockSpec((tm, wb), lambda i: (i, 0)),
            pl.BlockSpec((1, B_V), lambda i: (0, 0)),
            pl.BlockSpec((4, B_DIM), lambda i: (0, 0)),
        ],
        out_specs=pl.BlockSpec((tm, wo), lambda i: (i, 0)),
        out_shape=jax.ShapeDtypeStruct((n, wo), BF16),
        compiler_params=_params("parallel"),
        name="merge_ab",
    )(oa, ob, g_sub.reshape(1, B_V), lam_vec)


def _top_rows(s, k):
    vals = []
    cur = s
    for _ in range(k):
        mx = jnp.max(cur, axis=0, keepdims=True)
        vals.append(mx)
        cur = jnp.where(cur == mx, NEG_INF, cur)
    return jnp.concatenate(vals, axis=0)


def _peer_stats_kernel(q_ref, keys_ref, c1_ref, e2_ref, th_ref):
    dn = (((1,), (1,)), ((), ()))
    s1 = lax.dot_general(keys_ref[0], q_ref[:, 0:N_KEYS].astype(BF16), dn, preferred_element_type=F32)
    s2 = lax.dot_general(keys_ref[1], q_ref[:, N_KEYS:2 * N_KEYS].astype(BF16), dn, preferred_element_type=F32)
    v1 = _top_rows(s1, P_TOPK)
    v2 = _top_rows(s2, P_TOPK)
    n_b = [P_TOPK // (a + 1) for a in range(P_TOPK)]
    cand = jnp.concatenate([v1[a:a + 1] + v2[0:n_b[a]] for a in range(P_TOPK)], axis=0)
    best = _top_rows(cand, P_TOPK)
    inv_z = 1.0 / jnp.sum(jnp.exp(best - best[0:1]), axis=0, keepdims=True)
    c1_ref[0] = jnp.exp(s1 - v1[0:1]) * inv_z
    e2_ref[0] = jnp.exp(s2 - v2[0:1])
    c1v = jnp.exp(v1 - v1[0:1]) * inv_z
    e2v = jnp.exp(v2 - v2[0:1])
    gates = jnp.concatenate([c1v[a:a + 1] * e2v[0:n_b[a]] for a in range(P_TOPK)], axis=0)
    th_ref[0] = jnp.min(jnp.where(cand >= best[P_TOPK - 1:P_TOPK], gates, jnp.inf), axis=0, keepdims=True)


def _peer_stats(q, keys):
    n = q.shape[0]
    tm = _pick(n, (256, 128))
    big = pl.BlockSpec((1, N_KEYS, tm), lambda i, h: (h, 0, i))
    big_shape = jax.ShapeDtypeStruct((P_HEADS, N_KEYS, n), F32)
    return pl.pallas_call(
        _peer_stats_kernel,
        grid=(n // tm, P_HEADS),
        in_specs=[
            pl.BlockSpec((tm, P_DQ), lambda i, h: (i, h)),
            pl.BlockSpec((2, N_KEYS, P_DQ // 2), lambda i, h: (h, 0, 0)),
        ],
        out_specs=[big, big, pl.BlockSpec((1, 1, tm), lambda i, h: (h, 0, i))],
        out_shape=[big_shape, big_shape, jax.ShapeDtypeStruct((P_HEADS, 1, n), F32)],
        compiler_params=_params("parallel", "parallel"),
        name="peer_stats",
    )(q, keys)


def _gelu_tanh(a):
    return 0.5 * a * (1.0 + jnp.tanh(math.sqrt(2.0 / math.pi) * (a + 0.044715 * (a * a * a))))


def _peer_dense_kernel(h_ref, u_ref, vt_ref, c1_ref, e2_ref, th_ref, res_ref, gate_ref,
                       o_ref, acc_ref, at_ref, wt_ref, *, te, tm, tc):
    j = pl.program_id(1)

    @pl.when(j == 0)
    def _():
        acc_ref[...] = jnp.zeros(acc_ref.shape, F32)

    at_ref[...] = jnp.dot(u_ref[...], h_ref[...], preferred_element_type=F32)
    for i1 in range(te // N_KEYS):
        rows = slice(i1 * N_KEYS, (i1 + 1) * N_KEYS)
        for c in range(tm // tc):
            cols = slice(c * tc, (c + 1) * tc)
            m = jnp.zeros((N_KEYS, tc), F32)
            for h in range(P_HEADS):
                g = e2_ref[h, :, cols] * c1_ref[h, i1:i1 + 1, cols]
                m = m + jnp.where(g >= th_ref[h, :, cols], g, 0.0)
            wt_ref[rows, cols] = (m * _gelu_tanh(at_ref[rows, cols])).astype(BF16)
    acc_ref[...] += jnp.dot(vt_ref[...], wt_ref[...], preferred_element_type=F32)

    @pl.when(j == pl.num_programs(1) - 1)
    def _():
        o_ref[...] = res_ref[...] + gate_ref[0] * acc_ref[...].T


def _peer_dense(h, u, vt, layer, stats, res, gate):
    d, n = h.shape
    e = u.shape[1]
    c1, e2, theta = stats
    per_mod = n // gate.shape[0]
    tm = _pick(per_mod, (512, 256, 128))
    te = PEER_TE
    tc = _pick(tm, (256, 128))
    tiles_per_mod = per_mod // tm
    once = dict(pipeline_mode=pl.Buffered(1))
    big = pl.BlockSpec((P_HEADS, N_KEYS, tm), lambda i, j: (0, 0, i), **once)
    row = pl.BlockSpec((P_HEADS, te // N_KEYS, tm), lambda i, j: (0, j, i))
    return pl.pallas_call(
        functools.partial(_peer_dense_kernel, te=te, tm=tm, tc=tc),
        grid=(n // tm, e // te),
        in_specs=[
            pl.BlockSpec((d, tm), lambda i, j: (0, i)),
            pl.BlockSpec((None, te, d), lambda i, j: (layer, j, 0)),
            pl.BlockSpec((None, None, d, te), lambda i, j: (layer, j, 0, 0)),
            row, big,
            pl.BlockSpec((P_HEADS, 1, tm), lambda i, j: (0, 0, i)),
            pl.BlockSpec((tm, d), lambda i, j: (i, 0), **once),
            pl.BlockSpec((1, 1, d), lambda i, j: (i // tiles_per_mod, 0, 0)),
        ],
        out_specs=pl.BlockSpec((tm, d), lambda i, j: (i, 0), **once),
        out_shape=jax.ShapeDtypeStruct((n, d), F32),
        scratch_shapes=[pltpu.VMEM((d, tm), F32), pltpu.VMEM((te, tm), F32), pltpu.VMEM((te, tm), BF16)],
        compiler_params=_params("parallel", "arbitrary"),
        name="peer_dense",
    )(h, u, vt, c1, e2, theta, res, gate)


def _rope_tables(seq, rot_dim):
    n_freq = rot_dim // 4
    freqs = ROPE_BASE ** (-jnp.arange(n_freq, dtype=F32) / n_freq)
    pos = jnp.arange(seq)
    row = (pos // GRID_W).astype(F32)[:, None] * freqs
    col = (pos % GRID_W).astype(F32)[:, None] * freqs
    zero = jnp.zeros_like(row)
    cos = jnp.concatenate([jnp.cos(row), jnp.cos(row), jnp.cos(col), jnp.cos(col)], axis=-1)
    s_up = jnp.concatenate([-jnp.sin(row), zero, -jnp.sin(col), zero], axis=-1)
    s_dn = jnp.concatenate([zero, jnp.sin(row), zero, jnp.sin(col)], axis=-1)
    return cos, s_up, s_dn


def _tile_tables(tabs, reps, width):
    out = []
    for idx, t in enumerate(tabs):
        t = jnp.tile(t, (1, reps))
        pad = width - t.shape[1]
        if pad:
            fill = jnp.ones if idx == 0 else jnp.zeros
            t = jnp.concatenate([t, fill((t.shape[0], pad), F32)], axis=-1)
        out.append(t)
    return out


def _pad_lanes(v, width):
    return jnp.pad(v, (0, width - v.shape[0])).reshape(1, width)


def _layout_w_in_ab(w):
    d = w.shape[0]
    cq, ckv, kr, bq, bk, bv = jnp.split(w, [768, 1280, 1344, 2368, 3392], axis=1)
    return jnp.concatenate([cq, kr, jnp.zeros((d, AB_BQ - AB_KR - A_ROPE), w.dtype), bq, bk, bv, ckv],
                           axis=1).astype(BF16)


def _layout_w_uq(w):
    k = w.shape[0]
    w3 = w.reshape(k, A_HEADS, A_QK)
    nope = w3[:, :, :A_NOPE].reshape(k, A_HEADS * A_NOPE)
    rope = jnp.pad(w3[:, :, A_NOPE:], ((0, 0), (0, 0), (0, LANE_V7X - A_ROPE))).reshape(k, A_HEADS * LANE_V7X)
    return jnp.concatenate([nope, rope], axis=1).astype(BF16)


def _mixer_ab(tok, mod, p, tables, *, batch, seq, rope):
    proj = _nmm(tok, p["g_norm1"], p["w_in"], shift=mod[0], scale=mod[1], name="ab_in")
    qa_raw = _nmm(proj, p["g_cq"], p["w_uq"], k_block=AB_CQ // A_Q_LORA, name="ab_uq")
    kva_raw = _nmm(proj, p["g_ckv"], p["w_ukv"], k_block=AB_CKV // A_KV_LORA, name="ab_ukv")
    sa, sb = A_QK ** -0.5 * LOG2_E, B_DIM ** -0.5 * LOG2_E
    (qa,) = _prep_call(
        functools.partial(_prep_aq_kernel, rope=rope, scale=sa), [qa_raw], [(qa_raw.shape[1], 0)],
        [p["gqa_n"], p["gqa_r"]], tables["a"] if rope else None, [((A_HEADS, 1), 2 * LANE_V7X)],
        batch=batch, seq=seq, name="prep_aq")
    ka, va = _prep_call(
        functools.partial(_prep_akv_kernel, rope=rope), [kva_raw, proj],
        [(kva_raw.shape[1], 0), (LANE_V7X, AB_KR // LANE_V7X)],
        [p["gka_n"], p["gka_r"]], tables["a"] if rope else None,
        [((A_HEADS,), 2 * LANE_V7X), ((A_HEADS,), A_V)], batch=batch, seq=seq, name="prep_akv")
    wb = B_HEADS * 2 * B_DIM
    qb, kb, vb = _prep_call(
        functools.partial(_prep_b_kernel, rope=rope, scale=sb), [proj, proj, proj],
        [(wb, AB_BQ // wb), (wb, AB_BK // wb), (wb, AB_BV // wb)],
        [p["gqb"], p["gkb"]], tables["b"] if rope else None,
        [((2 * B_HEADS, 1), B_DIM), ((2 * B_HEADS,), B_DIM), ((B_HEADS,), B_V)],
        batch=batch, seq=seq, name="prep_b")
    return (qa, ka, va), (qb, kb, vb)


def _mixer_c(tok, mod, p, tables, *, batch, seq, rope):
    proj = _nmm(tok, p["g_norm1"], p["w_in"], shift=mod[0], scale=mod[1], name="c_in")
    wq = C_HEADS * C_DIM
    wkv = 2 * C_KV_HEADS * C_DIM
    return _prep_call(
        functools.partial(_prep_c_kernel, rope=rope, scale=C_DIM ** -0.5 * LOG2_E), [proj, proj],
        [(wq, 0), (wkv, wq // wkv)], [p["gq"], p["gk"]], tables["c"] if rope else None,
        [((C_KV_HEADS, C_GROUP), C_DIM), ((C_KV_HEADS,), C_DIM), ((C_KV_HEADS,), C_DIM)],
        batch=batch, seq=seq, name="prep_c")


def _peer(tok, mod, gate, p, name):
    q, h = _nmm(tok, p["g_norm2"], p["w_pq"], shift=mod[0], scale=mod[1], emit_h=True, name=name)
    stats = _peer_stats(q, p["keys"])
    return _peer_dense(h, p["u_all"], p["vt_all"], p["layer"], stats, tok, gate)


def kernel(x, c, ctx, c_ctx, w_mod, b_mod, g_norm1, g_norm2, w_in_ab, g_cq, w_uq, g_ckv, w_ukv, g_qn_a, g_kn_a,
           lam_vec, g_qn_b, g_kn_b, g_sub_b, w_out_ab, w_in_c, g_qn_c, g_kn_c, w_out_c, w_pq, sub_keys,
           expert_u, expert_v):
    batch, seq, d = x.shape
    ctx_len = ctx.shape[1]
    depth = w_mod.shape[0]
    assert batch + 1 <= 8

    cz = jnp.concatenate([c, c_ctx[None, :], jnp.zeros((8 - batch - 1, d), F32)], axis=0)
    mod_all = _modulation(cz, w_mod, b_mod).reshape(depth, 8, 6, d)

    tab64 = _rope_tables(seq, A_ROPE)
    tables = {
        "a": _tile_tables(tab64, 1, LANE_V7X),
        "b": _tile_tables(tab64, 2, LANE_V7X),
        "c": _rope_tables(seq, C_DIM),
    }

    xt = x.reshape(batch * seq, d)
    zt = ctx.reshape(batch * ctx_len, d)
    u_all = expert_u.astype(BF16)
    n_exp = expert_v.shape[1]
    vt_all = jnp.swapaxes(expert_v.reshape(depth, n_exp // PEER_TE, PEER_TE, d), 2, 3).astype(BF16)

    for layer in range(depth):
        last = layer == depth - 1
        e = layer // 2
        mx = [mod_all[layer, :batch, i][:, None, :] for i in range(6)]
        mz = [mod_all[layer, batch:batch + 1, i][:, None, :] for i in range(6)]
        pp = {
            "g_norm2": g_norm2[layer], "w_pq": w_pq[layer].astype(BF16),
            "keys": sub_keys[layer].reshape(2 * P_HEADS, N_KEYS, P_DQ // 2).astype(BF16),
            "u_all": u_all, "vt_all": vt_all, "layer": layer,
        }
        if layer % 2 == 0:
            lam_init = 0.8 - 0.6 * math.exp(-0.3 * layer)
            p = {
                "g_norm1": g_norm1[layer], "w_in": _layout_w_in_ab(w_in_ab[e]),
                "g_cq": g_cq[e], "w_uq": _layout_w_uq(w_uq[e]),
                "g_ckv": g_ckv[e], "w_ukv": w_ukv[e].astype(BF16),
                "gqa_n": g_qn_a[e][:A_NOPE].reshape(1, A_NOPE), "gqa_r": _pad_lanes(g_qn_a[e][A_NOPE:], LANE_V7X),
                "gka_n": g_kn_a[e][:A_NOPE].reshape(1, A_NOPE), "gka_r": _pad_lanes(g_kn_a[e][A_NOPE:], LANE_V7X),
                "gqb": jnp.tile(g_qn_b[e], 2).reshape(1, 2 * B_DIM), "gkb": jnp.tile(g_kn_b[e], 2).reshape(1, 2 * B_DIM),
            }
            w_out = w_out_ab[e].astype(BF16)
            (qa, ka, va), (qb, kb, vb) = _mixer_ab(xt, mx, p, tables, batch=batch, seq=seq, rope=True)
            (qaz, kaz, vaz), (qbz, kbz, vbz) = _mixer_ab(zt, mz, p, tables, batch=batch, seq=ctx_len, rope=False)
            oa = _flash(qa, [(ka, va), (kaz, vaz)], out_dtype=BF16, name="flash_a")
            ob = _flash(qb, [(kb, vb), (kbz, vbz)], out_dtype=F32, name="flash_b")
            att = _merge_ab(oa.reshape(batch * seq, -1), ob.reshape(batch * seq, -1), g_sub_b[e], lam_vec[e], lam_init)
            if not last:
                oaz = _flash(qaz, [(kaz, vaz)], out_dtype=BF16, name="flash_az")
                obz = _flash(qbz, [(kbz, vbz)], out_dtype=F32, name="flash_bz")
                attz = _merge_ab(oaz.reshape(batch * ctx_len, -1), obz.reshape(batch * ctx_len, -1),
                                 g_sub_b[e], lam_vec[e], lam_init)
        else:
            p = {
                "g_norm1": g_norm1[layer], "w_in": w_in_c[e].astype(BF16),
                "gq": g_qn_c[e].reshape(1, C_DIM), "gk": g_kn_c[e].reshape(1, C_DIM),
            }
            w_out = w_out_c[e].astype(BF16)
            q, k, v = _mixer_c(xt, mx, p, tables, batch=batch, seq=seq, rope=True)
            qz, kz, vz = _mixer_c(zt, mz, p, tables, batch=batch, seq=ctx_len, rope=False)
            att = _flash(q, [(k, v), (kz, vz)], out_dtype=BF16, name="flash_c").reshape(batch * seq, -1)
            if not last:
                attz = _flash(qz, [(kz, vz)], out_dtype=BF16, name="flash_cz").reshape(batch * ctx_len, -1)

        xt = _mm_res(att, w_out, xt, mx[2], name="out_proj")
        xt = _peer(xt, (mx[3], mx[4]), mx[5], pp, "peer_q")
        if not last:
            zt = _mm_res(attz, w_out, zt, mz[2], name="out_proj_z")
            zt = _peer(zt, (mz[3], mz[4]), mz[5], pp, "peer_qz")
    return xt.reshape(batch, seq, d)
```

```python
import functools
import math

import jax
import jax.numpy as jnp
from jax import lax
from jax.experimental import pallas as pl
from jax.experimental.pallas import tpu as pltpu

F32 = jnp.float32
BF16 = jnp.bfloat16

LANE_V7X = 128
VMEM_LIMIT_V7X = 56 * 1024 * 1024

GRID_W = 64
ROPE_BASE = 10000.0
EPS = 1e-6
A_HEADS, A_Q_LORA, A_KV_LORA, A_NOPE, A_ROPE, A_V = 8, 768, 512, 128, 64, 128
A_QK = A_NOPE + A_ROPE
B_HEADS, B_DIM, B_V = 8, 64, 128
C_HEADS, C_KV_HEADS, C_DIM = 16, 4, 128
C_GROUP = C_HEADS // C_KV_HEADS
P_HEADS, N_KEYS, P_DQ, P_TOPK = 8, 128, 256, 16
N_EXPERTS = N_KEYS * N_KEYS
PEER_TE = 1024
FLASH_TK = 256
FLASH_ROWS = 2048
NEG_INF = float("-inf")
LOG2_E = math.log2(math.e)

AB_CQ, AB_KR, AB_BQ, AB_BK, AB_BV, AB_CKV, AB_COLS = 0, 768, 1024, 2048, 3072, 4096, 4608


def _pick(n, prefs):
    for p in prefs:
        if n % p == 0:
            return p
    return n


def _col_tiles(w, tn):
    k, m = w.shape
    return w.reshape(k, m // tn, tn).transpose(1, 0, 2)


def _params(*sem):
    return pltpu.CompilerParams(dimension_semantics=sem, vmem_limit_bytes=VMEM_LIMIT_V7X)


def _mod_kernel(c_ref, w_ref, b_ref, o_ref):
    c = c_ref[...]
    s = c * (1.0 / (1.0 + jnp.exp(-c)))
    o_ref[0] = jnp.dot(s.astype(BF16), w_ref[0].astype(BF16), preferred_element_type=F32) + b_ref[0]


def _modulation(cz, w_mod, b_mod):
    depth, d, n = w_mod.shape
    tn = _pick(n, (1024, 768, 512, 256, 128))
    return pl.pallas_call(
        _mod_kernel,
        grid=(depth, n // tn),
        in_specs=[
            pl.BlockSpec((8, d), lambda l, j: (0, 0)),
            pl.BlockSpec((1, d, tn), lambda l, j: (l, 0, j)),
            pl.BlockSpec((1, 1, tn), lambda l, j: (l, 0, j)),
        ],
        out_specs=pl.BlockSpec((1, 8, tn), lambda l, j: (l, 0, j)),
        out_shape=jax.ShapeDtypeStruct((depth, 8, n), F32),
        compiler_params=_params("parallel", "parallel"),
        name="modulation",
    )(cz, w_mod, b_mod.reshape(depth, 1, n))


def _nmm_kernel(*refs, has_mod, emit_h):
    if has_mod:
        x_ref, g_ref, sh_ref, sc_ref, w_ref = refs[:5]
        rest = refs[5:]
    else:
        x_ref, g_ref, w_ref = refs[:3]
        rest = refs[3:]
    if emit_h:
        o_ref, ho_ref, h_scr = rest
    else:
        o_ref, h_scr = rest

    @pl.when(pl.program_id(1) == 0)
    def _():
        x = x_ref[...].astype(F32)
        y = x * lax.rsqrt(jnp.mean(x * x, axis=-1, keepdims=True) + EPS) * g_ref[...]
        if has_mod:
            y = y * (1.0 + sc_ref[0]) + sh_ref[0]
        h_scr[...] = y.astype(BF16)
        if emit_h:
            ho_ref[...] = y.T.astype(BF16)

    o_ref[...] = jnp.dot(h_scr[...], w_ref[...], preferred_element_type=F32).astype(o_ref.dtype)


def _nmm(x, g, w, *, k_block=0, shift=None, scale=None, emit_h=False, out_dtype=F32, name):
    n = x.shape[0]
    k, m = w.shape
    has_mod = shift is not None
    n_mod = shift.shape[0] if has_mod else 1
    per_mod = n // n_mod
    tm = _pick(per_mod, (1024, 512, 256, 128))
    tn = _pick(m, (512, 768, 256, 128))
    tiles_per_mod = per_mod // tm
    in_specs = [pl.BlockSpec((tm, k), lambda i, j: (i, k_block)), pl.BlockSpec((1, k), lambda i, j: (0, 0))]
    args = [x, g.reshape(1, k)]
    if has_mod:
        mod_spec = pl.BlockSpec((1, 1, k), lambda i, j: (i // tiles_per_mod, 0, 0))
        in_specs += [mod_spec, mod_spec]
        args += [shift, scale]
    in_specs.append(pl.BlockSpec((None, k, tn), lambda i, j: (j, 0, 0)))
    args.append(_col_tiles(w, tn))
    out_specs = pl.BlockSpec((tm, tn), lambda i, j: (i, j))
    out_shape = jax.ShapeDtypeStruct((n, m), out_dtype)
    if emit_h:
        out_specs = [out_specs, pl.BlockSpec((k, tm), lambda i, j: (0, i))]
        out_shape = [out_shape, jax.ShapeDtypeStruct((k, n), BF16)]
    return pl.pallas_call(
        functools.partial(_nmm_kernel, has_mod=has_mod, emit_h=emit_h),
        grid=(n // tm, m // tn),
        in_specs=in_specs,
        out_specs=out_specs,
        out_shape=out_shape,
        scratch_shapes=[pltpu.VMEM((tm, k), BF16)],
        compiler_params=_params("parallel", "arbitrary"),
        name=name,
    )(*args)


def _mm_res_kernel(a_ref, w_ref, res_ref, gate_ref, o_ref):
    acc = jnp.dot(a_ref[...], w_ref[...], preferred_element_type=F32)
    o_ref[...] = res_ref[...] + gate_ref[0] * acc


def _mm_res(a, w, res, gate, *, name):
    n, k = a.shape
    m = w.shape[1]
    per_mod = n // gate.shape[0]
    tm = _pick(per_mod, (1024, 512, 256, 128))
    tn = _pick(m, (512, 256, 128))
    tiles_per_mod = per_mod // tm
    return pl.pallas_call(
        _mm_res_kernel,
        grid=(n // tm, m // tn),
        in_specs=[
            pl.BlockSpec((tm, k), lambda i, j: (i, 0)),
            pl.BlockSpec((None, k, tn), lambda i, j: (j, 0, 0)),
            pl.BlockSpec((tm, tn), lambda i, j: (i, j)),
            pl.BlockSpec((1, 1, tn), lambda i, j: (i // tiles_per_mod, 0, j)),
        ],
        out_specs=pl.BlockSpec((tm, tn), lambda i, j: (i, j)),
        out_shape=jax.ShapeDtypeStruct((n, m), F32),
        compiler_params=_params("parallel", "parallel"),
        name=name,
    )(a, _col_tiles(w, tn), res, gate)


def _rope(y, cos, sin_up, sin_dn, n_freq):
    w = y.shape[-1]
    return y * cos + pltpu.roll(y, w - n_freq, 1) * sin_up + pltpu.roll(y, n_freq, 1) * sin_dn


def _sumsq(x):
    return jnp.sum(x * x, axis=-1, keepdims=True)


def _prep_aq_kernel(*refs, rope, scale):
    if rope:
        x_ref, gn_ref, gr_ref, cos_ref, su_ref, sd_ref, o_ref = refs
    else:
        x_ref, gn_ref, gr_ref, o_ref = refs
    gn, gr = gn_ref[...], gr_ref[...]
    for h in range(A_HEADS):
        xn = x_ref[:, h * 128:(h + 1) * 128]
        xr = x_ref[:, 1024 + h * 128:1024 + (h + 1) * 128]
        r = lax.rsqrt((_sumsq(xn) + _sumsq(xr)) * (1.0 / A_QK) + EPS)
        yn = xn * r * gn
        yr = xr * r * gr
        if rope:
            yr = _rope(yr, cos_ref[...], su_ref[...], sd_ref[...], A_ROPE // 4)
        o_ref[0, h, 0, :, 0:128] = (yn * scale).astype(BF16)
        o_ref[0, h, 0, :, 128:256] = (yr * scale).astype(BF16)


def _prep_akv_kernel(*refs, rope):
    if rope:
        x_ref, kr_ref, gn_ref, gr_ref, cos_ref, su_ref, sd_ref, k_ref, v_ref = refs
    else:
        x_ref, kr_ref, gn_ref, gr_ref, k_ref, v_ref = refs
    gn = gn_ref[...]
    kr = kr_ref[...]
    ss_r = _sumsq(kr)
    krg = kr * gr_ref[...]
    if rope:
        krg = _rope(krg, cos_ref[...], su_ref[...], sd_ref[...], A_ROPE // 4)
    for h in range(A_HEADS):
        xn = x_ref[:, h * 256:h * 256 + 128]
        r = lax.rsqrt((_sumsq(xn) + ss_r) * (1.0 / A_QK) + EPS)
        k_ref[0, h, :, 0:128] = (xn * r * gn).astype(BF16)
        k_ref[0, h, :, 128:256] = (krg * r).astype(BF16)
        v_ref[0, h] = x_ref[:, h * 256 + 128:(h + 1) * 256].astype(BF16)


def _norm64_pairs(x, g):
    x2 = x * x
    lo = lax.broadcasted_iota(jnp.int32, x.shape, 1) < B_DIM
    s_lo = jnp.sum(jnp.where(lo, x2, 0.0), axis=-1, keepdims=True)
    s_hi = jnp.sum(jnp.where(lo, 0.0, x2), axis=-1, keepdims=True)
    r = jnp.where(lo, lax.rsqrt(s_lo * (1.0 / B_DIM) + EPS), lax.rsqrt(s_hi * (1.0 / B_DIM) + EPS))
    return x * r * g


def _prep_b_kernel(*refs, rope, scale):
    if rope:
        q_ref, k_ref, v_ref, gq_ref, gk_ref, cos_ref, su_ref, sd_ref, qo_ref, ko_ref, vo_ref = refs
    else:
        q_ref, k_ref, v_ref, gq_ref, gk_ref, qo_ref, ko_ref, vo_ref = refs
    for src, g_ref, dst, mul in ((q_ref, gq_ref, qo_ref, scale), (k_ref, gk_ref, ko_ref, None)):
        g = g_ref[...]
        for j in range(B_HEADS):
            y = _norm64_pairs(src[:, j * 128:(j + 1) * 128], g)
            if rope:
                y = _rope(y, cos_ref[...], su_ref[...], sd_ref[...], B_DIM // 4)
            if mul is not None:
                y = y * mul
            yb = y.astype(BF16)
            if dst is qo_ref:
                dst[0, 2 * j, 0] = yb[:, 0:B_DIM]
                dst[0, 2 * j + 1, 0] = yb[:, B_DIM:2 * B_DIM]
            else:
                dst[0, 2 * j] = yb[:, 0:B_DIM]
                dst[0, 2 * j + 1] = yb[:, B_DIM:2 * B_DIM]
    for h in range(B_HEADS):
        vo_ref[0, h] = v_ref[:, h * B_V:(h + 1) * B_V].astype(BF16)


def _prep_c_kernel(*refs, rope, scale):
    if rope:
        q_ref, kv_ref, gq_ref, gk_ref, cos_ref, su_ref, sd_ref, qo_ref, ko_ref, vo_ref = refs
    else:
        q_ref, kv_ref, gq_ref, gk_ref, qo_ref, ko_ref, vo_ref = refs

    def norm_rope(x, g):
        y = x * lax.rsqrt(_sumsq(x) * (1.0 / C_DIM) + EPS) * g
        if rope:
            y = _rope(y, cos_ref[...], su_ref[...], sd_ref[...], C_DIM // 4)
        return y

    gq, gk = gq_ref[...], gk_ref[...]
    for h in range(C_HEADS):
        y = norm_rope(q_ref[:, h * C_DIM:(h + 1) * C_DIM], gq) * scale
        qo_ref[0, h // C_GROUP, h % C_GROUP] = y.astype(BF16)
    for h in range(C_KV_HEADS):
        ko_ref[0, h] = norm_rope(kv_ref[:, h * C_DIM:(h + 1) * C_DIM], gk).astype(BF16)
        vo_ref[0, h] = kv_ref[:, (C_KV_HEADS + h) * C_DIM:(C_KV_HEADS + h + 1) * C_DIM].astype(BF16)


def _prep_call(kernel, ins, in_blocks, consts, tables, outs, *, batch, seq, name):
    ts = _pick(seq, (256, 128))
    n_s = seq // ts
    in_specs = [pl.BlockSpec((ts, w), functools.partial(lambda b, i, c: (b * n_s + i, c), c=c)) for w, c in in_blocks]
    in_specs += [pl.BlockSpec(c.shape, lambda b, i: (0, 0)) for c in consts]
    if tables is not None:
        in_specs += [pl.BlockSpec((ts, LANE_V7X), lambda b, i: (i, 0)) for _ in tables]
    out_specs, out_shape = [], []
    for heads, d in outs:
        nz = len(heads)
        out_specs.append(pl.BlockSpec((1,) + heads + (ts, d),
                                      functools.partial(lambda b, i, nz: (b,) + (0,) * nz + (i, 0), nz=nz)))
        out_shape.append(jax.ShapeDtypeStruct((batch,) + heads + (seq, d), BF16))
    args = list(ins) + list(consts) + (list(tables) if tables is not None else [])
    return pl.pallas_call(
        kernel,
        grid=(batch, n_s),
        in_specs=in_specs,
        out_specs=out_specs,
        out_shape=out_shape,
        compiler_params=_params("parallel", "parallel"),
        name=name,
    )(*args)


def _flash_kernel(*refs, group, tq, n_src, tks):
    q_ref, kv_refs, o_ref = refs[0], refs[1:1 + 2 * n_src], refs[1 + 2 * n_src]
    m_scr, l_scr, acc_scr = refs[2 + 2 * n_src:]
    dv = LANE_V7X
    m_scr[...] = jnp.full(m_scr.shape, NEG_INF, F32)
    l_scr[...] = jnp.zeros(l_scr.shape, F32)
    acc_scr[...] = jnp.zeros(acc_scr.shape, F32)

    q = q_ref[0, 0].reshape(group * tq, q_ref.shape[-1])
    for src in range(n_src):
        k_ref, v_ref = kv_refs[2 * src], kv_refs[2 * src + 1]
        tk = tks[src]
        ones = jnp.ones((tk, LANE_V7X), BF16)
        for t in range(k_ref.shape[2] // tk):
            k = k_ref[0, 0, t * tk:(t + 1) * tk, :]
            v1 = jnp.concatenate([v_ref[0, 0, t * tk:(t + 1) * tk, :], ones], axis=1)
            s = lax.dot_general(q, k, (((1,), (1,)), ((), ())), preferred_element_type=F32)
            m_prev = m_scr[...]
            m_new = jnp.maximum(m_prev, jnp.max(s, axis=-1, keepdims=True))
            alpha = jnp.exp2(m_prev - m_new)
            p = jnp.exp2(s - jnp.concatenate([m_new] * (tk // LANE_V7X), axis=1))
            pv = jnp.dot(p.astype(BF16), v1, preferred_element_type=F32)
            acc_scr[...] = alpha * acc_scr[...] + pv[:, :dv]
            l_scr[...] = alpha * l_scr[...] + pv[:, dv:]
            m_scr[...] = m_new

    out = acc_scr[...] / l_scr[...]
    for g in range(group):
        o_ref[0, :, g * dv:(g + 1) * dv] = out[g * tq:(g + 1) * tq].astype(o_ref.dtype)


def _flash(q, kv_sources, *, out_dtype, name):
    b, hk, group, s, d = q.shape
    hv, dv = kv_sources[0][1].shape[1], kv_sources[0][1].shape[3]
    assert dv == LANE_V7X
    k_per_v = hk // hv
    tq = _pick(s, (FLASH_ROWS // group, 256 // group, 128 // group))
    tks = tuple(_pick(k.shape[2], (FLASH_TK, 256, LANE_V7X)) for k, _ in kv_sources)
    rows = group * tq
    stat = pltpu.VMEM((rows, LANE_V7X), F32)
    in_specs = [pl.BlockSpec((1, 1, group, tq, d), lambda bi, h, i: (bi, h, 0, i, 0))]
    args = [q]
    for k, v in kv_sources:
        t = k.shape[2]
        in_specs += [pl.BlockSpec((1, 1, t, d), lambda bi, h, i: (bi, h, 0, 0)),
                     pl.BlockSpec((1, 1, t, dv), lambda bi, h, i: (bi, h // k_per_v, 0, 0))]
        args += [k, v]
    return pl.pallas_call(
        functools.partial(_flash_kernel, group=group, tq=tq, n_src=len(kv_sources), tks=tks),
        grid=(b, hk, s // tq),
        in_specs=in_specs,
        out_specs=pl.BlockSpec((1, tq, group * dv), lambda bi, h, i: (bi, i, h)),
        out_shape=jax.ShapeDtypeStruct((b, s, hk * group * dv), out_dtype),
        scratch_shapes=[stat, stat, stat],
        compiler_params=_params("parallel", "parallel", "parallel"),
        name=name,
    )(*args)


def _merge_ab_kernel(oa_ref, ob_ref, g_ref, lv_ref, o_ref, *, lam_init):
    lv = lv_ref[...]
    lam = (jnp.exp(jnp.sum(lv[0:1] * lv[1:2], axis=-1, keepdims=True))
           - jnp.exp(jnp.sum(lv[2:3] * lv[3:4], axis=-1, keepdims=True)) + lam_init)
    n_a = A_HEADS * A_V
    o_ref[:, 0:n_a] = oa_ref[...].astype(BF16)
    g = g_ref[...]
    for h in range(B_HEADS):
        o = ob_ref[:, 2 * h * B_V:(2 * h + 1) * B_V] - lam * ob_ref[:, (2 * h + 1) * B_V:(2 * h + 2) * B_V]
        y = o * lax.rsqrt(_sumsq(o) * (1.0 / B_V) + EPS) * g * (1.0 - lam_init)
        o_ref[:, n_a + h * B_V:n_a + (h + 1) * B_V] = y.astype(BF16)


def _merge_ab(oa, ob, g_sub, lam_vec, lam_init):
    n = oa.shape[0]
    tm = _pick(n, (512, 256, 128))
    wa, wb = oa.shape[1], ob.shape[1]
    wo = wa + B_HEADS * B_V
    return pl.pallas_call(
        functools.partial(_merge_ab_kernel, lam_init=lam_init),
        grid=(n // tm,),
        in_specs=[
            pl.BlockSpec((tm, wa), lambda i: (i, 0)),
            pl.BlockSpec((tm, wb), lambda i: (i, 0)),
            pl.BlockSpec((1, B_V), lambda i: (0, 0)),
            pl.BlockSpec((4, B_DIM), lambda i: (0, 0)),
        ],
        out_specs=pl.BlockSpec((tm, wo), lambda i: (i, 0)),
        out_shape=jax.ShapeDtypeStruct((n, wo), BF16),
        compiler_params=_params("parallel"),
        name="merge_ab",
    )(oa, ob, g_sub.reshape(1, B_V), lam_vec)


def _top_rows(s, k):
    vals = []
    cur = s
    for _ in range(k):
        mx = jnp.max(cur, axis=0, keepdims=True)
        vals.append(mx)
        cur = jnp.where(cur == mx, NEG_INF, cur)
    return jnp.concatenate(vals, axis=0)


def _sort_desc(a, bitonic):
    n = len(a)
    a = list(a)
    for k in ((n,) if bitonic else [2 << e for e in range(n.bit_length() - 1)]):
        j = k // 2
        while j >= 1:
            for i in range(n):
                l = i ^ j
                if l > i:
                    hi, lo = jnp.maximum(a[i], a[l]), jnp.minimum(a[i], a[l])
                    a[i], a[l] = (hi, lo) if (i & k) == 0 or bitonic else (lo, hi)
            j //= 2
    return a


def _top16_of_128(s):
    n = P_TOPK
    a = _sort_desc([s[8 * i:8 * i + 8] for i in range(n)], False)
    for shift in (4, 2, 1):
        a = _sort_desc([jnp.maximum(a[i], pltpu.roll(a[n - 1 - i], shift, 0)) for i in range(n)], True)
    return jnp.concatenate([x[0:1] for x in a], axis=0)


def _peer_stats_kernel(q_ref, keys_ref, c1_ref, e2_ref, th_ref):
    dn = (((1,), (1,)), ((), ()))
    s1 = lax.dot_general(keys_ref[0], q_ref[:, 0:N_KEYS].astype(BF16), dn, preferred_element_type=F32)
    s2 = lax.dot_general(keys_ref[1], q_ref[:, N_KEYS:2 * N_KEYS].astype(BF16), dn, preferred_element_type=F32)
    v1 = _top16_of_128(s1)
    v2 = _top16_of_128(s2)
    n_b = [P_TOPK // (a + 1) for a in range(P_TOPK)]
    cand = jnp.concatenate([v1[a:a + 1] + v2[0:n_b[a]] for a in range(P_TOPK)], axis=0)
    best = _top_rows(cand, P_TOPK)
    inv_z = 1.0 / jnp.sum(jnp.exp(best - best[0:1]), axis=0, keepdims=True)
    c1_ref[0] = jnp.exp(s1 - v1[0:1]) * inv_z
    e2_ref[0] = jnp.exp(s2 - v2[0:1])
    c1v = jnp.exp(v1 - v1[0:1]) * inv_z
    e2v = jnp.exp(v2 - v2[0:1])
    gates = jnp.concatenate([c1v[a:a + 1] * e2v[0:n_b[a]] for a in range(P_TOPK)], axis=0)
    th_ref[0] = jnp.min(jnp.where(cand >= best[P_TOPK - 1:P_TOPK], gates, jnp.inf), axis=0, keepdims=True)


def _peer_stats(q, keys):
    n = q.shape[0]
    tm = _pick(n, (256, 128))
    big = pl.BlockSpec((1, N_KEYS, tm), lambda i, h: (h, 0, i))
    big_shape = jax.ShapeDtypeStruct((P_HEADS, N_KEYS, n), F32)
    return pl.pallas_call(
        _peer_stats_kernel,
        grid=(n // tm, P_HEADS),
        in_specs=[
            pl.BlockSpec((tm, P_DQ), lambda i, h: (i, h)),
            pl.BlockSpec((2, N_KEYS, P_DQ // 2), lambda i, h: (h, 0, 0)),
        ],
        out_specs=[big, big, pl.BlockSpec((1, 1, tm), lambda i, h: (h, 0, i))],
        out_shape=[big_shape, big_shape, jax.ShapeDtypeStruct((P_HEADS, 1, n), F32)],
        compiler_params=_params("parallel", "parallel"),
        name="peer_stats",
    )(q, keys)


def _gelu_tanh(a):
    return 0.5 * a * (1.0 + jnp.tanh(math.sqrt(2.0 / math.pi) * (a + 0.044715 * (a * a * a))))


def _peer_dense_kernel(h_ref, u_ref, vt_ref, c1_ref, e2_ref, th_ref, res_ref, gate_ref,
                       o_ref, acc_ref, at_ref, wt_ref, *, te, tm, tc):
    j = pl.program_id(1)

    @pl.when(j == 0)
    def _():
        acc_ref[...] = jnp.zeros(acc_ref.shape, F32)

    at_ref[...] = jnp.dot(u_ref[...], h_ref[...], preferred_element_type=F32)
    for i1 in range(te // N_KEYS):
        rows = slice(i1 * N_KEYS, (i1 + 1) * N_KEYS)
        for c in range(tm // tc):
            cols = slice(c * tc, (c + 1) * tc)
            m = jnp.zeros((N_KEYS, tc), F32)
            for h in range(P_HEADS):
                g = e2_ref[h, :, cols] * c1_ref[h, i1:i1 + 1, cols]
                m = m + jnp.where(g >= th_ref[h, :, cols], g, 0.0)
            wt_ref[rows, cols] = (m * _gelu_tanh(at_ref[rows, cols])).astype(BF16)
    acc_ref[...] += jnp.dot(vt_ref[...], wt_ref[...], preferred_element_type=F32)

    @pl.when(j == pl.num_programs(1) - 1)
    def _():
        o_ref[...] = res_ref[...] + gate_ref[0] * acc_ref[...].T


def _peer_dense(h, u, vt, layer, stats, res, gate):
    d, n = h.shape
    e = u.shape[1]
    c1, e2, theta = stats
    per_mod = n // gate.shape[0]
    tm = _pick(per_mod, (512, 256, 128))
    te = PEER_TE
    tc = _pick(tm, (256, 128))
    tiles_per_mod = per_mod // tm
    once = dict(pipeline_mode=pl.Buffered(1))
    big = pl.BlockSpec((P_HEADS, N_KEYS, tm), lambda i, j: (0, 0, i), **once)
    row = pl.BlockSpec((P_HEADS, te // N_KEYS, tm), lambda i, j: (0, j, i))
    return pl.pallas_call(
        functools.partial(_peer_dense_kernel, te=te, tm=tm, tc=tc),
        grid=(n // tm, e // te),
        in_specs=[
            pl.BlockSpec((d, tm), lambda i, j: (0, i)),
            pl.BlockSpec((None, te, d), lambda i, j: (layer, j, 0)),
            pl.BlockSpec((None, None, d, te), lambda i, j: (layer, j, 0, 0)),
            row, big,
            pl.BlockSpec((P_HEADS, 1, tm), lambda i, j: (0, 0, i)),
            pl.BlockSpec((tm, d), lambda i, j: (i, 0), **once),
            pl.BlockSpec((1, 1, d), lambda i, j: (i // tiles_per_mod, 0, 0)),
        ],
        out_specs=pl.BlockSpec((tm, d), lambda i, j: (i, 0), **once),
        out_shape=jax.ShapeDtypeStruct((n, d), F32),
        scratch_shapes=[pltpu.VMEM((d, tm), F32), pltpu.VMEM((te, tm), F32), pltpu.VMEM((te, tm), BF16)],
        compiler_params=_params("parallel", "arbitrary"),
        name="peer_dense",
    )(h, u, vt, c1, e2, theta, res, gate)


def _rope_tables(seq, rot_dim):
    n_freq = rot_dim // 4
    freqs = ROPE_BASE ** (-jnp.arange(n_freq, dtype=F32) / n_freq)
    pos = jnp.arange(seq)
    row = (pos // GRID_W).astype(F32)[:, None] * freqs
    col = (pos % GRID_W).astype(F32)[:, None] * freqs
    zero = jnp.zeros_like(row)
    cos = jnp.concatenate([jnp.cos(row), jnp.cos(row), jnp.cos(col), jnp.cos(col)], axis=-1)
    s_up = jnp.concatenate([-jnp.sin(row), zero, -jnp.sin(col), zero], axis=-1)
    s_dn = jnp.concatenate([zero, jnp.sin(row), zero, jnp.sin(col)], axis=-1)
    return cos, s_up, s_dn


def _tile_tables(tabs, reps, width):
    out = []
    for idx, t in enumerate(tabs):
        t = jnp.tile(t, (1, reps))
        pad = width - t.shape[1]
        if pad:
            fill = jnp.ones if idx == 0 else jnp.zeros
            t = jnp.concatenate([t, fill((t.shape[0], pad), F32)], axis=-1)
        out.append(t)
    return out


def _pad_lanes(v, width):
    return jnp.pad(v, (0, width - v.shape[0])).reshape(1, width)


def _layout_w_in_ab(w):
    d = w.shape[0]
    cq, ckv, kr, bq, bk, bv = jnp.split(w, [768, 1280, 1344, 2368, 3392], axis=1)
    return jnp.concatenate([cq, kr, jnp.zeros((d, AB_BQ - AB_KR - A_ROPE), w.dtype), bq, bk, bv, ckv],
                           axis=1).astype(BF16)


def _layout_w_uq(w):
    k = w.shape[0]
    w3 = w.reshape(k, A_HEADS, A_QK)
    nope = w3[:, :, :A_NOPE].reshape(k, A_HEADS * A_NOPE)
    rope = jnp.pad(w3[:, :, A_NOPE:], ((0, 0), (0, 0), (0, LANE_V7X - A_ROPE))).reshape(k, A_HEADS * LANE_V7X)
    return jnp.concatenate([nope, rope], axis=1).astype(BF16)


def _mixer_ab(tok, mod, p, tables, *, batch, seq, rope):
    proj = _nmm(tok, p["g_norm1"], p["w_in"], shift=mod[0], scale=mod[1], name="ab_in")
    qa_raw = _nmm(proj, p["g_cq"], p["w_uq"], k_block=AB_CQ // A_Q_LORA, name="ab_uq")
    kva_raw = _nmm(proj, p["g_ckv"], p["w_ukv"], k_block=AB_CKV // A_KV_LORA, name="ab_ukv")
    sa, sb = A_QK ** -0.5 * LOG2_E, B_DIM ** -0.5 * LOG2_E
    (qa,) = _prep_call(
        functools.partial(_prep_aq_kernel, rope=rope, scale=sa), [qa_raw], [(qa_raw.shape[1], 0)],
        [p["gqa_n"], p["gqa_r"]], tables["a"] if rope else None, [((A_HEADS, 1), 2 * LANE_V7X)],
        batch=batch, seq=seq, name="prep_aq")
    ka, va = _prep_call(
        functools.partial(_prep_akv_kernel, rope=rope), [kva_raw, proj],
        [(kva_raw.shape[1], 0), (LANE_V7X, AB_KR // LANE_V7X)],
        [p["gka_n"], p["gka_r"]], tables["a"] if rope else None,
        [((A_HEADS,), 2 * LANE_V7X), ((A_HEADS,), A_V)], batch=batch, seq=seq, name="prep_akv")
    wb = B_HEADS * 2 * B_DIM
    qb, kb, vb = _prep_call(
        functools.partial(_prep_b_kernel, rope=rope, scale=sb), [proj, proj, proj],
        [(wb, AB_BQ // wb), (wb, AB_BK // wb), (wb, AB_BV // wb)],
        [p["gqb"], p["gkb"]], tables["b"] if rope else None,
        [((2 * B_HEADS, 1), B_DIM), ((2 * B_HEADS,), B_DIM), ((B_HEADS,), B_V)],
        batch=batch, seq=seq, name="prep_b")
    return (qa, ka, va), (qb, kb, vb)


def _mixer_c(tok, mod, p, tables, *, batch, seq, rope):
    proj = _nmm(tok, p["g_norm1"], p["w_in"], shift=mod[0], scale=mod[1], name="c_in")
    wq = C_HEADS * C_DIM
    wkv = 2 * C_KV_HEADS * C_DIM
    return _prep_call(
        functools.partial(_prep_c_kernel, rope=rope, scale=C_DIM ** -0.5 * LOG2_E), [proj, proj],
        [(wq, 0), (wkv, wq // wkv)], [p["gq"], p["gk"]], tables["c"] if rope else None,
        [((C_KV_HEADS, C_GROUP), C_DIM), ((C_KV_HEADS,), C_DIM), ((C_KV_HEADS,), C_DIM)],
        batch=batch, seq=seq, name="prep_c")


def _peer(tok, mod, gate, p, name):
    q, h = _nmm(tok, p["g_norm2"], p["w_pq"], shift=mod[0], scale=mod[1], emit_h=True, name=name)
    stats = _peer_stats(q, p["keys"])
    return _peer_dense(h, p["u_all"], p["vt_all"], p["layer"], stats, tok, gate)


def kernel(x, c, ctx, c_ctx, w_mod, b_mod, g_norm1, g_norm2, w_in_ab, g_cq, w_uq, g_ckv, w_ukv, g_qn_a, g_kn_a,
           lam_vec, g_qn_b, g_kn_b, g_sub_b, w_out_ab, w_in_c, g_qn_c, g_kn_c, w_out_c, w_pq, sub_keys,
           expert_u, expert_v):
    batch, seq, d = x.shape
    ctx_len = ctx.shape[1]
    depth = w_mod.shape[0]
    assert batch + 1 <= 8

    cz = jnp.concatenate([c, c_ctx[None, :], jnp.zeros((8 - batch - 1, d), F32)], axis=0)
    mod_all = _modulation(cz, w_mod, b_mod).reshape(depth, 8, 6, d)

    tab64 = _rope_tables(seq, A_ROPE)
    tables = {
        "a": _tile_tables(tab64, 1, LANE_V7X),
        "b": _tile_tables(tab64, 2, LANE_V7X),
        "c": _rope_tables(seq, C_DIM),
    }

    xt = x.reshape(batch * seq, d)
    zt = ctx.reshape(batch * ctx_len, d)
    u_all = expert_u.astype(BF16)
    n_exp = expert_v.shape[1]
    vt_all = jnp.swapaxes(expert_v.reshape(depth, n_exp // PEER_TE, PEER_TE, d), 2, 3).astype(BF16)

    for layer in range(depth):
        last = layer == depth - 1
        e = layer // 2
        mx = [mod_all[layer, :batch, i][:, None, :] for i in range(6)]
        mz = [mod_all[layer, batch:batch + 1, i][:, None, :] for i in range(6)]
        pp = {
            "g_norm2": g_norm2[layer], "w_pq": w_pq[layer].astype(BF16),
            "keys": sub_keys[layer].reshape(2 * P_HEADS, N_KEYS, P_DQ // 2).astype(BF16),
            "u_all": u_all, "vt_all": vt_all, "layer": layer,
        }
        if layer % 2 == 0:
            lam_init = 0.8 - 0.6 * math.exp(-0.3 * layer)
            p = {
                "g_norm1": g_norm1[layer], "w_in": _layout_w_in_ab(w_in_ab[e]),
                "g_cq": g_cq[e], "w_uq": _layout_w_uq(w_uq[e]),
                "g_ckv": g_ckv[e], "w_ukv": w_ukv[e].astype(BF16),
                "gqa_n": g_qn_a[e][:A_NOPE].reshape(1, A_NOPE), "gqa_r": _pad_lanes(g_qn_a[e][A_NOPE:], LANE_V7X),
                "gka_n": g_kn_a[e][:A_NOPE].reshape(1, A_NOPE), "gka_r": _pad_lanes(g_kn_a[e][A_NOPE:], LANE_V7X),
                "gqb": jnp.tile(g_qn_b[e], 2).reshape(1, 2 * B_DIM), "gkb": jnp.tile(g_kn_b[e], 2).reshape(1, 2 * B_DIM),
            }
            w_out = w_out_ab[e].astype(BF16)
            (qa, ka, va), (qb, kb, vb) = _mixer_ab(xt, mx, p, tables, batch=batch, seq=seq, rope=True)
            (qaz, kaz, vaz), (qbz, kbz, vbz) = _mixer_ab(zt, mz, p, tables, batch=batch, seq=ctx_len, rope=False)
            oa = _flash(qa, [(ka, va), (kaz, vaz)], out_dtype=BF16, name="flash_a")
            ob = _flash(qb, [(kb, vb), (kbz, vbz)], out_dtype=F32, name="flash_b")
            att = _merge_ab(oa.reshape(batch * seq, -1), ob.reshape(batch * seq, -1), g_sub_b[e], lam_vec[e], lam_init)
            if not last:
                oaz = _flash(qaz, [(kaz, vaz)], out_dtype=BF16, name="flash_az")
                obz = _flash(qbz, [(kbz, vbz)], out_dtype=F32, name="flash_bz")
                attz = _merge_ab(oaz.reshape(batch * ctx_len, -1), obz.reshape(batch * ctx_len, -1),
                                 g_sub_b[e], lam_vec[e], lam_init)
        else:
            p = {
                "g_norm1": g_norm1[layer], "w_in": w_in_c[e].astype(BF16),
                "gq": g_qn_c[e].reshape(1, C_DIM), "gk": g_kn_c[e].reshape(1, C_DIM),
            }
            w_out = w_out_c[e].astype(BF16)
            q, k, v = _mixer_c(xt, mx, p, tables, batch=batch, seq=seq, rope=True)
            qz, kz, vz = _mixer_c(zt, mz, p, tables, batch=batch, seq=ctx_len, rope=False)
            att = _flash(q, [(k, v), (kz, vz)], out_dtype=BF16, name="flash_c").reshape(batch * seq, -1)
            if not last:
                attz = _flash(qz, [(kz, vz)], out_dtype=BF16, name="flash_cz").reshape(batch * ctx_len, -1)

        xt = _mm_res(att, w_out, xt, mx[2], name="out_proj")
        xt = _peer(xt, (mx[3], mx[4]), mx[5], pp, "peer_q")
        if not last:
            zt = _mm_res(attz, w_out, zt, mz[2], name="out_proj_z")
            zt = _peer(zt, (mz[3], mz[4]), mz[5], pp, "peer_qz")
    return xt.reshape(batch, seq, d)
```

```python
import functools
import math

import jax
import jax.numpy as jnp
from jax import lax
from jax.experimental import pallas as pl
from jax.experimental.pallas import tpu as pltpu

F32 = jnp.float32
BF16 = jnp.bfloat16

LANE_V7X = 128
VMEM_LIMIT_V7X = 56 * 1024 * 1024

GRID_W = 64
ROPE_BASE = 10000.0
EPS = 1e-6
A_HEADS, A_Q_LORA, A_KV_LORA, A_NOPE, A_ROPE, A_V = 8, 768, 512, 128, 64, 128
A_QK = A_NOPE + A_ROPE
B_HEADS, B_DIM, B_V = 8, 64, 128
C_HEADS, C_KV_HEADS, C_DIM = 16, 4, 128
C_GROUP = C_HEADS // C_KV_HEADS
P_HEADS, N_KEYS, P_DQ, P_TOPK = 8, 128, 256, 16
PEER_TE = 1024
FLASH_TK = 256
FLASH_ROWS = 2048
NEG_INF = float("-inf")
LOG2_E = math.log2(math.e)

AB_CQ, AB_KR, AB_BQ, AB_BK, AB_BV, AB_CKV = 0, 768, 1024, 2048, 3072, 4096


def _pick(n, prefs):
    for p in prefs:
        if n % p == 0:
            return p
    return n


def _col_tiles(w, tn):
    k, m = w.shape
    return w.reshape(k, m // tn, tn).transpose(1, 0, 2)


def _params(*sem):
    return pltpu.CompilerParams(dimension_semantics=sem, vmem_limit_bytes=VMEM_LIMIT_V7X)


def _mod_kernel(c_ref, w_ref, b_ref, o_ref):
    c = c_ref[...]
    s = c * (1.0 / (1.0 + jnp.exp(-c)))
    o_ref[0] = jnp.dot(s.astype(BF16), w_ref[0].astype(BF16), preferred_element_type=F32) + b_ref[0]


def _modulation(cz, w_mod, b_mod):
    depth, d, n = w_mod.shape
    tn = _pick(n, (1024, 768, 512, 256, 128))
    return pl.pallas_call(
        _mod_kernel,
        grid=(depth, n // tn),
        in_specs=[
            pl.BlockSpec((8, d), lambda l, j: (0, 0)),
            pl.BlockSpec((1, d, tn), lambda l, j: (l, 0, j)),
            pl.BlockSpec((1, 1, tn), lambda l, j: (l, 0, j)),
        ],
        out_specs=pl.BlockSpec((1, 8, tn), lambda l, j: (l, 0, j)),
        out_shape=jax.ShapeDtypeStruct((depth, 8, n), F32),
        compiler_params=_params("parallel", "parallel"),
        name="modulation",
    )(cz, w_mod, b_mod.reshape(depth, 1, n))


def _nmm_kernel(*refs, has_mod, emit_h):
    if has_mod:
        x_ref, g_ref, sh_ref, sc_ref, w_ref = refs[:5]
        rest = refs[5:]
    else:
        x_ref, g_ref, w_ref = refs[:3]
        rest = refs[3:]
    if emit_h:
        o_ref, ho_ref, h_scr = rest
    else:
        o_ref, h_scr = rest

    @pl.when(pl.program_id(1) == 0)
    def _():
        x = x_ref[...].astype(F32)
        y = x * lax.rsqrt(jnp.mean(x * x, axis=-1, keepdims=True) + EPS) * g_ref[...]
        if has_mod:
            y = y * (1.0 + sc_ref[0]) + sh_ref[0]
        h_scr[...] = y.astype(BF16)
        if emit_h:
            ho_ref[...] = y.T.astype(BF16)

    o_ref[...] = jnp.dot(h_scr[...], w_ref[...], preferred_element_type=F32).astype(o_ref.dtype)


def _nmm(x, g, w, *, k_block=0, shift=None, scale=None, emit_h=False, out_dtype=F32, name):
    n = x.shape[0]
    k, m = w.shape
    has_mod = shift is not None
    n_mod = shift.shape[0] if has_mod else 1
    per_mod = n // n_mod
    tm = _pick(per_mod, (1024, 512, 256, 128))
    tn = _pick(m, (512, 768, 256, 128))
    tiles_per_mod = per_mod // tm
    in_specs = [pl.BlockSpec((tm, k), lambda i, j: (i, k_block)), pl.BlockSpec((1, k), lambda i, j: (0, 0))]
    args = [x, g.reshape(1, k)]
    if has_mod:
        mod_spec = pl.BlockSpec((1, 1, k), lambda i, j: (i // tiles_per_mod, 0, 0))
        in_specs += [mod_spec, mod_spec]
        args += [shift, scale]
    in_specs.append(pl.BlockSpec((None, k, tn), lambda i, j: (j, 0, 0)))
    args.append(_col_tiles(w, tn))
    out_specs = pl.BlockSpec((tm, tn), lambda i, j: (i, j))
    out_shape = jax.ShapeDtypeStruct((n, m), out_dtype)
    if emit_h:
        out_specs = [out_specs, pl.BlockSpec((k, tm), lambda i, j: (0, i))]
        out_shape = [out_shape, jax.ShapeDtypeStruct((k, n), BF16)]
    return pl.pallas_call(
        functools.partial(_nmm_kernel, has_mod=has_mod, emit_h=emit_h),
        grid=(n // tm, m // tn),
        in_specs=in_specs,
        out_specs=out_specs,
        out_shape=out_shape,
        scratch_shapes=[pltpu.VMEM((tm, k), BF16)],
        compiler_params=_params("parallel", "arbitrary"),
        name=name,
    )(*args)


def _mm_res_kernel(a_ref, w_ref, res_ref, gate_ref, o_ref):
    acc = jnp.dot(a_ref[...], w_ref[...], preferred_element_type=F32)
    o_ref[...] = res_ref[...] + gate_ref[0] * acc


def _mm_res(a, w, res, gate, *, name):
    n, k = a.shape
    m = w.shape[1]
    per_mod = n // gate.shape[0]
    tm = _pick(per_mod, (1024, 512, 256, 128))
    tn = _pick(m, (512, 256, 128))
    tiles_per_mod = per_mod // tm
    return pl.pallas_call(
        _mm_res_kernel,
        grid=(n // tm, m // tn),
        in_specs=[
            pl.BlockSpec((tm, k), lambda i, j: (i, 0)),
            pl.BlockSpec((None, k, tn), lambda i, j: (j, 0, 0)),
            pl.BlockSpec((tm, tn), lambda i, j: (i, j)),
            pl.BlockSpec((1, 1, tn), lambda i, j: (i // tiles_per_mod, 0, j)),
        ],
        out_specs=pl.BlockSpec((tm, tn), lambda i, j: (i, j)),
        out_shape=jax.ShapeDtypeStruct((n, m), F32),
        compiler_params=_params("parallel", "parallel"),
        name=name,
    )(a, _col_tiles(w, tn), res, gate)


def _rope(y, cos, sin_up, sin_dn, n_freq):
    w = y.shape[-1]
    return y * cos + pltpu.roll(y, w - n_freq, 1) * sin_up + pltpu.roll(y, n_freq, 1) * sin_dn


def _sumsq(x):
    return jnp.sum(x * x, axis=-1, keepdims=True)


def _prep_aq_kernel(*refs, rope, scale):
    if rope:
        x_ref, gn_ref, gr_ref, cos_ref, su_ref, sd_ref, o_ref = refs
    else:
        x_ref, gn_ref, gr_ref, o_ref = refs
    gn, gr = gn_ref[...], gr_ref[...]
    for h in range(A_HEADS):
        xn = x_ref[:, h * 128:(h + 1) * 128]
        xr = x_ref[:, 1024 + h * 128:1024 + (h + 1) * 128]
        r = lax.rsqrt((_sumsq(xn) + _sumsq(xr)) * (1.0 / A_QK) + EPS)
        yn = xn * r * gn
        yr = xr * r * gr
        if rope:
            yr = _rope(yr, cos_ref[...], su_ref[...], sd_ref[...], A_ROPE // 4)
        o_ref[0, h, 0, :, 0:128] = (yn * scale).astype(BF16)
        o_ref[0, h, 0, :, 128:256] = (yr * scale).astype(BF16)


def _prep_akv_kernel(*refs, rope):
    if rope:
        x_ref, kr_ref, gn_ref, gr_ref, cos_ref, su_ref, sd_ref, k_ref, v_ref = refs
    else:
        x_ref, kr_ref, gn_ref, gr_ref, k_ref, v_ref = refs
    gn = gn_ref[...]
    kr = kr_ref[...]
    ss_r = _sumsq(kr)
    krg = kr * gr_ref[...]
    if rope:
        krg = _rope(krg, cos_ref[...], su_ref[...], sd_ref[...], A_ROPE // 4)
    for h in range(A_HEADS):
        xn = x_ref[:, h * 256:h * 256 + 128]
        r = lax.rsqrt((_sumsq(xn) + ss_r) * (1.0 / A_QK) + EPS)
        k_ref[0, h, :, 0:128] = (xn * r * gn).astype(BF16)
        k_ref[0, h, :, 128:256] = (krg * r).astype(BF16)
        v_ref[0, h] = x_ref[:, h * 256 + 128:(h + 1) * 256].astype(BF16)


def _norm64_pairs(x, g, lo):
    x2 = x * x
    s_lo = jnp.sum(jnp.where(lo, x2, 0.0), axis=-1, keepdims=True)
    s_hi = jnp.sum(jnp.where(lo, 0.0, x2), axis=-1, keepdims=True)
    r = jnp.where(lo, lax.rsqrt(s_lo * (1.0 / B_DIM) + EPS), lax.rsqrt(s_hi * (1.0 / B_DIM) + EPS))
    return x * r * g


def _prep_b_kernel(*refs, rope, scale):
    if rope:
        q_ref, k_ref, v_ref, gq_ref, gk_ref, cos_ref, su_ref, sd_ref, qo_ref, ko_ref, vo_ref = refs
    else:
        q_ref, k_ref, v_ref, gq_ref, gk_ref, qo_ref, ko_ref, vo_ref = refs
    lo = lax.broadcasted_iota(jnp.int32, (q_ref.shape[0], 2 * B_DIM), 1) < B_DIM
    for src, g_ref, dst, mul in ((q_ref, gq_ref, qo_ref, scale), (k_ref, gk_ref, ko_ref, None)):
        g = g_ref[...]
        for j in range(B_HEADS):
            y = _norm64_pairs(src[:, j * 128:(j + 1) * 128], g, lo)
            if rope:
                y = _rope(y, cos_ref[...], su_ref[...], sd_ref[...], B_DIM // 4)
            if mul is not None:
                y = y * mul
            yb = y.astype(BF16)
            if dst is qo_ref:
                dst[0, 2 * j, 0] = yb[:, 0:B_DIM]
                dst[0, 2 * j + 1, 0] = yb[:, B_DIM:2 * B_DIM]
            else:
                dst[0, 2 * j] = yb[:, 0:B_DIM]
                dst[0, 2 * j + 1] = yb[:, B_DIM:2 * B_DIM]
    for h in range(B_HEADS):
        vo_ref[0, h] = v_ref[:, h * B_V:(h + 1) * B_V].astype(BF16)


def _prep_c_kernel(*refs, rope, scale):
    if rope:
        q_ref, kv_ref, gq_ref, gk_ref, cos_ref, su_ref, sd_ref, qo_ref, ko_ref, vo_ref = refs
    else:
        q_ref, kv_ref, gq_ref, gk_ref, qo_ref, ko_ref, vo_ref = refs

    def norm_rope(x, g):
        y = x * lax.rsqrt(_sumsq(x) * (1.0 / C_DIM) + EPS) * g
        if rope:
            y = _rope(y, cos_ref[...], su_ref[...], sd_ref[...], C_DIM // 4)
        return y

    gq, gk = gq_ref[...], gk_ref[...]
    for h in range(C_HEADS):
        y = norm_rope(q_ref[:, h * C_DIM:(h + 1) * C_DIM], gq) * scale
        qo_ref[0, h // C_GROUP, h % C_GROUP] = y.astype(BF16)
    for h in range(C_KV_HEADS):
        ko_ref[0, h] = norm_rope(kv_ref[:, h * C_DIM:(h + 1) * C_DIM], gk).astype(BF16)
        vo_ref[0, h] = kv_ref[:, (C_KV_HEADS + h) * C_DIM:(C_KV_HEADS + h + 1) * C_DIM].astype(BF16)


def _prep_call(kernel, ins, in_blocks, consts, tables, outs, *, batch, seq, name):
    ts = _pick(seq, (256, 128))
    n_s = seq // ts
    in_specs = [pl.BlockSpec((ts, w), functools.partial(lambda b, i, c: (b * n_s + i, c), c=c)) for w, c in in_blocks]
    in_specs += [pl.BlockSpec(c.shape, lambda b, i: (0, 0)) for c in consts]
    if tables is not None:
        in_specs += [pl.BlockSpec((ts, LANE_V7X), lambda b, i: (i, 0)) for _ in tables]
    out_specs, out_shape = [], []
    for heads, d in outs:
        nz = len(heads)
        out_specs.append(pl.BlockSpec((1,) + heads + (ts, d),
                                      functools.partial(lambda b, i, nz: (b,) + (0,) * nz + (i, 0), nz=nz)))
        out_shape.append(jax.ShapeDtypeStruct((batch,) + heads + (seq, d), BF16))
    args = list(ins) + list(consts) + (list(tables) if tables is not None else [])
    return pl.pallas_call(
        kernel,
        grid=(batch, n_s),
        in_specs=in_specs,
        out_specs=out_specs,
        out_shape=out_shape,
        compiler_params=_params("parallel", "parallel"),
        name=name,
    )(*args)


def _flash_kernel(*refs, group, tq, n_src, tks):
    q_ref, kv_refs, o_ref = refs[0], refs[1:1 + 2 * n_src], refs[1 + 2 * n_src]
    m_scr, l_scr, acc_scr = refs[2 + 2 * n_src:]
    dv = LANE_V7X
    m_scr[...] = jnp.full(m_scr.shape, NEG_INF, F32)
    l_scr[...] = jnp.zeros(l_scr.shape, F32)
    acc_scr[...] = jnp.zeros(acc_scr.shape, F32)

    q = q_ref[0, 0].reshape(group * tq, q_ref.shape[-1])
    for src in range(n_src):
        k_ref, v_ref = kv_refs[2 * src], kv_refs[2 * src + 1]
        tk = tks[src]
        ones = jnp.ones((tk, LANE_V7X), BF16)
        for t in range(k_ref.shape[2] // tk):
            k = k_ref[0, 0, t * tk:(t + 1) * tk, :]
            v1 = jnp.concatenate([v_ref[0, 0, t * tk:(t + 1) * tk, :], ones], axis=1)
            s = lax.dot_general(q, k, (((1,), (1,)), ((), ())), preferred_element_type=F32)
            m_prev = m_scr[...]
            m_new = jnp.maximum(m_prev, jnp.max(s, axis=-1, keepdims=True))
            alpha = jnp.exp2(m_prev - m_new)
            p = jnp.exp2(s - jnp.concatenate([m_new] * (tk // LANE_V7X), axis=1))
            pv = jnp.dot(p.astype(BF16), v1, preferred_element_type=F32)
            acc_scr[...] = alpha * acc_scr[...] + pv[:, :dv]
            l_scr[...] = alpha * l_scr[...] + pv[:, dv:]
            m_scr[...] = m_new

    out = acc_scr[...] / l_scr[...]
    for g in range(group):
        o_ref[0, :, g * dv:(g + 1) * dv] = out[g * tq:(g + 1) * tq].astype(o_ref.dtype)


def _flash(q, kv_sources, *, out_dtype, name):
    b, hk, group, s, d = q.shape
    hv, dv = kv_sources[0][1].shape[1], kv_sources[0][1].shape[3]
    assert dv == LANE_V7X
    k_per_v = hk // hv
    tq = _pick(s, (FLASH_ROWS // group, 256 // group, 128 // group))
    tks = tuple(_pick(k.shape[2], (FLASH_TK, 256, LANE_V7X)) for k, _ in kv_sources)
    rows = group * tq
    stat = pltpu.VMEM((rows, LANE_V7X), F32)
    in_specs = [pl.BlockSpec((1, 1, group, tq, d), lambda bi, h, i: (bi, h, 0, i, 0))]
    args = [q]
    for k, v in kv_sources:
        t = k.shape[2]
        in_specs += [pl.BlockSpec((1, 1, t, d), lambda bi, h, i: (bi, h, 0, 0)),
                     pl.BlockSpec((1, 1, t, dv), lambda bi, h, i: (bi, h // k_per_v, 0, 0))]
        args += [k, v]
    return pl.pallas_call(
        functools.partial(_flash_kernel, group=group, tq=tq, n_src=len(kv_sources), tks=tks),
        grid=(b, hk, s // tq),
        in_specs=in_specs,
        out_specs=pl.BlockSpec((1, tq, group * dv), lambda bi, h, i: (bi, i, h)),
        out_shape=jax.ShapeDtypeStruct((b, s, hk * group * dv), out_dtype),
        scratch_shapes=[stat, stat, stat],
        compiler_params=_params("parallel", "parallel", "parallel"),
        name=name,
    )(*args)


def _merge_ab_kernel(oa_ref, ob_ref, g_ref, lv_ref, o_ref, *, lam_init):
    lv = lv_ref[...]
    lam = (jnp.exp(jnp.sum(lv[0:1] * lv[1:2], axis=-1, keepdims=True))
           - jnp.exp(jnp.sum(lv[2:3] * lv[3:4], axis=-1, keepdims=True)) + lam_init)
    n_a = A_HEADS * A_V
    o_ref[:, 0:n_a] = oa_ref[...].astype(BF16)
    g = g_ref[...]
    for h in range(B_HEADS):
        o = ob_ref[:, 2 * h * B_V:(2 * h + 1) * B_V] - lam * ob_ref[:, (2 * h + 1) * B_V:(2 * h + 2) * B_V]
        y = o * lax.rsqrt(_sumsq(o) * (1.0 / B_V) + EPS) * g * (1.0 - lam_init)
        o_ref[:, n_a + h * B_V:n_a + (h + 1) * B_V] = y.astype(BF16)


def _merge_ab(oa, ob, g_sub, lam_vec, lam_init):
    n = oa.shape[0]
    tm = _pick(n, (512, 256, 128))
    wa, wb = oa.shape[1], ob.shape[1]
    wo = wa + B_HEADS * B_V
    return pl.pallas_call(
        functools.partial(_merge_ab_kernel, lam_init=lam_init),
        grid=(n // tm,),
        in_specs=[
            pl.BlockSpec((tm, wa), lambda i: (i, 0)),
            pl.BlockSpec((tm, wb), lambda i: (i, 0)),
            pl.BlockSpec((1, B_V), lambda i: (0, 0)),
            pl.BlockSpec((4, B_DIM), lambda i: (0, 0)),
        ],
        out_specs=pl.BlockSpec((tm, wo), lambda i: (i, 0)),
        out_shape=jax.ShapeDtypeStruct((n, wo), BF16),
        compiler_params=_params("parallel"),
        name="merge_ab",
    )(oa, ob, g_sub.reshape(1, B_V), lam_vec)


def _top_rows(s, k):
    vals = []
    cur = s
    for _ in range(k):
        mx = jnp.max(cur, axis=0, keepdims=True)
        vals.append(mx)
        cur = jnp.where(cur == mx, NEG_INF, cur)
    return jnp.concatenate(vals, axis=0)


def _sort_desc(a, bitonic):
    n = len(a)
    a = list(a)
    for k in ((n,) if bitonic else [2 << e for e in range(n.bit_length() - 1)]):
        j = k // 2
        while j >= 1:
            for i in range(n):
                l = i ^ j
                if l > i:
                    hi, lo = jnp.maximum(a[i], a[l]), jnp.minimum(a[i], a[l])
                    a[i], a[l] = (hi, lo) if (i & k) == 0 or bitonic else (lo, hi)
            j //= 2
    return a


def _top16_of_128(s):
    n = P_TOPK
    a = _sort_desc([s[8 * i:8 * i + 8] for i in range(n)], False)
    for shift in (4, 2, 1):
        a = _sort_desc([jnp.maximum(a[i], pltpu.roll(a[n - 1 - i], shift, 0)) for i in range(n)], True)
    return jnp.concatenate([x[0:1] for x in a], axis=0)


def _peer_stats_kernel(q_ref, keys_ref, c1_ref, e2_ref, th_ref):
    dn = (((1,), (1,)), ((), ()))
    s1 = lax.dot_general(keys_ref[0], q_ref[:, 0:N_KEYS], dn, preferred_element_type=F32)
    s2 = lax.dot_general(keys_ref[1], q_ref[:, N_KEYS:2 * N_KEYS], dn, preferred_element_type=F32)
    v1 = _top16_of_128(s1)
    v2 = _top16_of_128(s2)
    n_b = [P_TOPK // (a + 1) for a in range(P_TOPK)]
    cand = jnp.concatenate([v1[a:a + 1] + v2[0:n_b[a]] for a in range(P_TOPK)], axis=0)
    best = _top_rows(cand, P_TOPK)
    inv_z = 1.0 / jnp.sum(jnp.exp(best - best[0:1]), axis=0, keepdims=True)
    c1_ref[0] = jnp.exp(s1 - v1[0:1]) * inv_z
    e2_ref[0] = jnp.exp(s2 - v2[0:1])
    c1v = jnp.exp(v1 - v1[0:1]) * inv_z
    e2v = jnp.exp(v2 - v2[0:1])
    gates = jnp.concatenate([c1v[a:a + 1] * e2v[0:n_b[a]] for a in range(P_TOPK)], axis=0)
    th_ref[0] = jnp.min(jnp.where(cand >= best[P_TOPK - 1:P_TOPK], gates, jnp.inf), axis=0, keepdims=True)


def _peer_stats(q, keys):
    n = q.shape[0]
    tm = _pick(n, (256, 128))
    big = pl.BlockSpec((1, N_KEYS, tm), lambda i, h: (h, 0, i))
    big_shape = jax.ShapeDtypeStruct((P_HEADS, N_KEYS, n), F32)
    return pl.pallas_call(
        _peer_stats_kernel,
        grid=(n // tm, P_HEADS),
        in_specs=[
            pl.BlockSpec((tm, P_DQ), lambda i, h: (i, h)),
            pl.BlockSpec((2, N_KEYS, P_DQ // 2), lambda i, h: (h, 0, 0)),
        ],
        out_specs=[big, big, pl.BlockSpec((1, 1, tm), lambda i, h: (h, 0, i))],
        out_shape=[big_shape, big_shape, jax.ShapeDtypeStruct((P_HEADS, 1, n), F32)],
        compiler_params=_params("parallel", "parallel"),
        name="peer_stats",
    )(q, keys)


def _gelu_tanh(a):
    return 0.5 * a * (1.0 + jnp.tanh(math.sqrt(2.0 / math.pi) * (a + 0.044715 * (a * a * a))))


def _peer_dense_kernel(h_ref, u_ref, vt_ref, c1_ref, e2_ref, th_ref, res_ref, gate_ref,
                       o_ref, acc_ref, at_ref, wt_ref, *, te, tm, tc):
    j = pl.program_id(1)

    @pl.when(j == 0)
    def _():
        acc_ref[...] = jnp.zeros(acc_ref.shape, F32)

    at_ref[...] = jnp.dot(u_ref[...], h_ref[...], preferred_element_type=F32)
    for i1 in range(te // N_KEYS):
        rows = slice(i1 * N_KEYS, (i1 + 1) * N_KEYS)
        for c in range(tm // tc):
            cols = slice(c * tc, (c + 1) * tc)
            m = jnp.zeros((N_KEYS, tc), F32)
            for h in range(P_HEADS):
                g = e2_ref[h, :, cols] * c1_ref[h, i1:i1 + 1, cols]
                m = m + jnp.where(g >= th_ref[h, :, cols], g, 0.0)
            wt_ref[rows, cols] = (m * _gelu_tanh(at_ref[rows, cols])).astype(BF16)
    acc_ref[...] += jnp.dot(vt_ref[...], wt_ref[...], preferred_element_type=F32)

    @pl.when(j == pl.num_programs(1) - 1)
    def _():
        o_ref[...] = res_ref[...] + gate_ref[0] * acc_ref[...].T


def _peer_dense(h, u, vt, layer, stats, res, gate):
    d, n = h.shape
    e = u.shape[1]
    c1, e2, theta = stats
    per_mod = n // gate.shape[0]
    tm = _pick(per_mod, (512, 256, 128))
    te = PEER_TE
    tc = _pick(tm, (256, 128))
    tiles_per_mod = per_mod // tm
    once = dict(pipeline_mode=pl.Buffered(1))
    big = pl.BlockSpec((P_HEADS, N_KEYS, tm), lambda i, j: (0, 0, i), **once)
    row = pl.BlockSpec((P_HEADS, te // N_KEYS, tm), lambda i, j: (0, j, i))
    return pl.pallas_call(
        functools.partial(_peer_dense_kernel, te=te, tm=tm, tc=tc),
        grid=(n // tm, e // te),
        in_specs=[
            pl.BlockSpec((d, tm), lambda i, j: (0, i)),
            pl.BlockSpec((None, te, d), lambda i, j: (layer, j, 0)),
            pl.BlockSpec((None, None, d, te), lambda i, j: (layer, j, 0, 0)),
            row, big,
            pl.BlockSpec((P_HEADS, 1, tm), lambda i, j: (0, 0, i)),
            pl.BlockSpec((tm, d), lambda i, j: (i, 0), **once),
            pl.BlockSpec((1, 1, d), lambda i, j: (i // tiles_per_mod, 0, 0)),
        ],
        out_specs=pl.BlockSpec((tm, d), lambda i, j: (i, 0), **once),
        out_shape=jax.ShapeDtypeStruct((n, d), F32),
        scratch_shapes=[pltpu.VMEM((d, tm), F32), pltpu.VMEM((te, tm), F32), pltpu.VMEM((te, tm), BF16)],
        compiler_params=_params("parallel", "arbitrary"),
        name="peer_dense",
    )(h, u, vt, c1, e2, theta, res, gate)


def _rope_tables(seq, rot_dim):
    n_freq = rot_dim // 4
    freqs = ROPE_BASE ** (-jnp.arange(n_freq, dtype=F32) / n_freq)
    pos = jnp.arange(seq)
    row = (pos // GRID_W).astype(F32)[:, None] * freqs
    col = (pos % GRID_W).astype(F32)[:, None] * freqs
    zero = jnp.zeros_like(row)
    cos = jnp.concatenate([jnp.cos(row), jnp.cos(row), jnp.cos(col), jnp.cos(col)], axis=-1)
    s_up = jnp.concatenate([-jnp.sin(row), zero, -jnp.sin(col), zero], axis=-1)
    s_dn = jnp.concatenate([zero, jnp.sin(row), zero, jnp.sin(col)], axis=-1)
    return cos, s_up, s_dn


def _tile_tables(tabs, reps, width):
    out = []
    for idx, t in enumerate(tabs):
        t = jnp.tile(t, (1, reps))
        pad = width - t.shape[1]
        if pad:
            fill = jnp.ones if idx == 0 else jnp.zeros
            t = jnp.concatenate([t, fill((t.shape[0], pad), F32)], axis=-1)
        out.append(t)
    return out


def _pad_lanes(v, width):
    return jnp.pad(v, (0, width - v.shape[0])).reshape(1, width)


def _layout_w_in_ab(w):
    d = w.shape[0]
    cq, ckv, kr, bq, bk, bv = jnp.split(w, [768, 1280, 1344, 2368, 3392], axis=1)
    return jnp.concatenate([cq, kr, jnp.zeros((d, AB_BQ - AB_KR - A_ROPE), w.dtype), bq, bk, bv, ckv],
                           axis=1).astype(BF16)


def _layout_w_uq(w):
    k = w.shape[0]
    w3 = w.reshape(k, A_HEADS, A_QK)
    nope = w3[:, :, :A_NOPE].reshape(k, A_HEADS * A_NOPE)
    rope = jnp.pad(w3[:, :, A_NOPE:], ((0, 0), (0, 0), (0, LANE_V7X - A_ROPE))).reshape(k, A_HEADS * LANE_V7X)
    return jnp.concatenate([nope, rope], axis=1).astype(BF16)


def _mixer_ab(tok, mod, p, tables, *, batch, seq, rope):
    proj = _nmm(tok, p["g_norm1"], p["w_in"], shift=mod[0], scale=mod[1], name="ab_in")
    qa_raw = _nmm(proj, p["g_cq"], p["w_uq"], k_block=AB_CQ // A_Q_LORA, name="ab_uq")
    kva_raw = _nmm(proj, p["g_ckv"], p["w_ukv"], k_block=AB_CKV // A_KV_LORA, name="ab_ukv")
    sa, sb = A_QK ** -0.5 * LOG2_E, B_DIM ** -0.5 * LOG2_E
    (qa,) = _prep_call(
        functools.partial(_prep_aq_kernel, rope=rope, scale=sa), [qa_raw], [(qa_raw.shape[1], 0)],
        [p["gqa_n"], p["gqa_r"]], tables["a"] if rope else None, [((A_HEADS, 1), 2 * LANE_V7X)],
        batch=batch, seq=seq, name="prep_aq")
    ka, va = _prep_call(
        functools.partial(_prep_akv_kernel, rope=rope), [kva_raw, proj],
        [(kva_raw.shape[1], 0), (LANE_V7X, AB_KR // LANE_V7X)],
        [p["gka_n"], p["gka_r"]], tables["a"] if rope else None,
        [((A_HEADS,), 2 * LANE_V7X), ((A_HEADS,), A_V)], batch=batch, seq=seq, name="prep_akv")
    wb = B_HEADS * 2 * B_DIM
    qb, kb, vb = _prep_call(
        functools.partial(_prep_b_kernel, rope=rope, scale=sb), [proj, proj, proj],
        [(wb, AB_BQ // wb), (wb, AB_BK // wb), (wb, AB_BV // wb)],
        [p["gqb"], p["gkb"]], tables["b"] if rope else None,
        [((2 * B_HEADS, 1), B_DIM), ((2 * B_HEADS,), B_DIM), ((B_HEADS,), B_V)],
        batch=batch, seq=seq, name="prep_b")
    return (qa, ka, va), (qb, kb, vb)


def _mixer_c(tok, mod, p, tables, *, batch, seq, rope):
    proj = _nmm(tok, p["g_norm1"], p["w_in"], shift=mod[0], scale=mod[1], name="c_in")
    wq = C_HEADS * C_DIM
    wkv = 2 * C_KV_HEADS * C_DIM
    return _prep_call(
        functools.partial(_prep_c_kernel, rope=rope, scale=C_DIM ** -0.5 * LOG2_E), [proj, proj],
        [(wq, 0), (wkv, wq // wkv)], [p["gq"], p["gk"]], tables["c"] if rope else None,
        [((C_KV_HEADS, C_GROUP), C_DIM), ((C_KV_HEADS,), C_DIM), ((C_KV_HEADS,), C_DIM)],
        batch=batch, seq=seq, name="prep_c")


def _peer(tok, mod, gate, p, name):
    q, h = _nmm(tok, p["g_norm2"], p["w_pq"], shift=mod[0], scale=mod[1], emit_h=True, out_dtype=BF16, name=name)
    stats = _peer_stats(q, p["keys"])
    return _peer_dense(h, p["u_all"], p["vt_all"], p["layer"], stats, tok, gate)


def kernel(x, c, ctx, c_ctx, w_mod, b_mod, g_norm1, g_norm2, w_in_ab, g_cq, w_uq, g_ckv, w_ukv, g_qn_a, g_kn_a,
           lam_vec, g_qn_b, g_kn_b, g_sub_b, w_out_ab, w_in_c, g_qn_c, g_kn_c, w_out_c, w_pq, sub_keys,
           expert_u, expert_v):
    batch, seq, d = x.shape
    ctx_len = ctx.shape[1]
    depth = w_mod.shape[0]
    assert batch + 1 <= 8

    cz = jnp.concatenate([c, c_ctx[None, :], jnp.zeros((8 - batch - 1, d), F32)], axis=0)
    mod_all = _modulation(cz, w_mod, b_mod).reshape(depth, 8, 6, d)

    tab64 = _rope_tables(seq, A_ROPE)
    tables = {
        "a": _tile_tables(tab64, 1, LANE_V7X),
        "b": _tile_tables(tab64, 2, LANE_V7X),
        "c": _rope_tables(seq, C_DIM),
    }

    xt = x.reshape(batch * seq, d)
    zt = ctx.reshape(batch * ctx_len, d)
    u_all = expert_u.astype(BF16)
    n_exp = expert_v.shape[1]
    vt_all = jnp.swapaxes(expert_v.reshape(depth, n_exp // PEER_TE, PEER_TE, d), 2, 3).astype(BF16)

    for layer in range(depth):
        last = layer == depth - 1
        e = layer // 2
        mx = [mod_all[layer, :batch, i][:, None, :] for i in range(6)]
        mz = [mod_all[layer, batch:batch + 1, i][:, None, :] for i in range(6)]
        pp = {
            "g_norm2": g_norm2[layer], "w_pq": w_pq[layer].astype(BF16),
            "keys": sub_keys[layer].reshape(2 * P_HEADS, N_KEYS, P_DQ // 2).astype(BF16),
            "u_all": u_all, "vt_all": vt_all, "layer": layer,
        }
        if layer % 2 == 0:
            lam_init = 0.8 - 0.6 * math.exp(-0.3 * layer)
            p = {
                "g_norm1": g_norm1[layer], "w_in": _layout_w_in_ab(w_in_ab[e]),
                "g_cq": g_cq[e], "w_uq": _layout_w_uq(w_uq[e]),
                "g_ckv": g_ckv[e], "w_ukv": w_ukv[e].astype(BF16),
                "gqa_n": g_qn_a[e][:A_NOPE].reshape(1, A_NOPE), "gqa_r": _pad_lanes(g_qn_a[e][A_NOPE:], LANE_V7X),
                "gka_n": g_kn_a[e][:A_NOPE].reshape(1, A_NOPE), "gka_r": _pad_lanes(g_kn_a[e][A_NOPE:], LANE_V7X),
                "gqb": jnp.tile(g_qn_b[e], 2).reshape(1, 2 * B_DIM), "gkb": jnp.tile(g_kn_b[e], 2).reshape(1, 2 * B_DIM),
            }
            w_out = w_out_ab[e].astype(BF16)
            (qa, ka, va), (qb, kb, vb) = _mixer_ab(xt, mx, p, tables, batch=batch, seq=seq, rope=True)
            (qaz, kaz, vaz), (qbz, kbz, vbz) = _mixer_ab(zt, mz, p, tables, batch=batch, seq=ctx_len, rope=False)
            oa = _flash(qa, [(ka, va), (kaz, vaz)], out_dtype=BF16, name="flash_a")
            ob = _flash(qb, [(kb, vb), (kbz, vbz)], out_dtype=F32, name="flash_b")
            att = _merge_ab(oa.reshape(batch * seq, -1), ob.reshape(batch * seq, -1), g_sub_b[e], lam_vec[e], lam_init)
            if not last:
                oaz = _flash(qaz, [(kaz, vaz)], out_dtype=BF16, name="flash_az")
                obz = _flash(qbz, [(kbz, vbz)], out_dtype=F32, name="flash_bz")
                attz = _merge_ab(oaz.reshape(batch * ctx_len, -1), obz.reshape(batch * ctx_len, -1),
                                 g_sub_b[e], lam_vec[e], lam_init)
        else:
            p = {
                "g_norm1": g_norm1[layer], "w_in": w_in_c[e].astype(BF16),
                "gq": g_qn_c[e].reshape(1, C_DIM), "gk": g_kn_c[e].reshape(1, C_DIM),
            }
            w_out = w_out_c[e].astype(BF16)
            q, k, v = _mixer_c(xt, mx, p, tables, batch=batch, seq=seq, rope=True)
            qz, kz, vz = _mixer_c(zt, mz, p, tables, batch=batch, seq=ctx_len, rope=False)
            att = _flash(q, [(k, v), (kz, vz)], out_dtype=BF16, name="flash_c").reshape(batch * seq, -1)
            if not last:
                attz = _flash(qz, [(kz, vz)], out_dtype=BF16, name="flash_cz").reshape(batch * ctx_len, -1)

        xt = _mm_res(att, w_out, xt, mx[2], name="out_proj")
        xt = _peer(xt, (mx[3], mx[4]), mx[5], pp, "peer_q")
        if not last:
            zt = _mm_res(attz, w_out, zt, mz[2], name="out_proj_z")
            zt = _peer(zt, (mz[3], mz[4]), mz[5], pp, "peer_qz")
    return xt.reshape(batch, seq, d)
```

```python
import functools
import math

import jax
import jax.numpy as jnp
from jax import lax
from jax.experimental import pallas as pl
from jax.experimental.pallas import tpu as pltpu

F32 = jnp.float32
BF16 = jnp.bfloat16

LANE_V7X = 128
VMEM_LIMIT_V7X = 56 * 1024 * 1024

GRID_W = 64
ROPE_BASE = 10000.0
EPS = 1e-6
A_HEADS, A_Q_LORA, A_KV_LORA, A_NOPE, A_ROPE, A_V = 8, 768, 512, 128, 64, 128
A_QK = A_NOPE + A_ROPE
B_HEADS, B_DIM, B_V = 8, 64, 128
C_HEADS, C_KV_HEADS, C_DIM = 16, 4, 128
C_GROUP = C_HEADS // C_KV_HEADS
P_HEADS, N_KEYS, P_DQ, P_TOPK = 8, 128, 256, 16
PEER_TE = 1024
FLASH_TK = 256
FLASH_ROWS = 2048
NEG_INF = float("-inf")
LOG2_E = math.log2(math.e)

AB_CQ, AB_KR, AB_BQ, AB_BK, AB_BV, AB_CKV = 0, 768, 1024, 2048, 3072, 4096


def _pick(n, prefs):
    for p in prefs:
        if n % p == 0:
            return p
    return n


def _col_tiles(w, tn):
    k, m = w.shape
    return w.reshape(k, m // tn, tn).transpose(1, 0, 2)


def _params(*sem):
    return pltpu.CompilerParams(dimension_semantics=sem, vmem_limit_bytes=VMEM_LIMIT_V7X)


def _mod_kernel(c_ref, w_ref, b_ref, o_ref):
    c = c_ref[...]
    s = c * (1.0 / (1.0 + jnp.exp(-c)))
    o_ref[0] = jnp.dot(s.astype(BF16), w_ref[0].astype(BF16), preferred_element_type=F32) + b_ref[0]


def _modulation(cz, w_mod, b_mod):
    depth, d, n = w_mod.shape
    tn = _pick(n, (1024, 768, 512, 256, 128))
    return pl.pallas_call(
        _mod_kernel,
        grid=(depth, n // tn),
        in_specs=[
            pl.BlockSpec((8, d), lambda l, j: (0, 0)),
            pl.BlockSpec((1, d, tn), lambda l, j: (l, 0, j)),
            pl.BlockSpec((1, 1, tn), lambda l, j: (l, 0, j)),
        ],
        out_specs=pl.BlockSpec((1, 8, tn), lambda l, j: (l, 0, j)),
        out_shape=jax.ShapeDtypeStruct((depth, 8, n), F32),
        compiler_params=_params("parallel", "parallel"),
        name="modulation",
    )(cz, w_mod, b_mod.reshape(depth, 1, n))


def _nmm_kernel(*refs, has_mod, emit_h):
    if has_mod:
        x_ref, g_ref, sh_ref, sc_ref, w_ref = refs[:5]
        rest = refs[5:]
    else:
        x_ref, g_ref, w_ref = refs[:3]
        rest = refs[3:]
    if emit_h:
        o_ref, ho_ref, h_scr = rest
    else:
        o_ref, h_scr = rest

    @pl.when(pl.program_id(1) == 0)
    def _():
        x = x_ref[...].astype(F32)
        y = x * lax.rsqrt(jnp.mean(x * x, axis=-1, keepdims=True) + EPS) * g_ref[...]
        if has_mod:
            y = y * (1.0 + sc_ref[0]) + sh_ref[0]
        h_scr[...] = y.astype(BF16)
        if emit_h:
            ho_ref[...] = y.T.astype(BF16)

    o_ref[...] = jnp.dot(h_scr[...], w_ref[...], preferred_element_type=F32).astype(o_ref.dtype)


def _nmm(x, g, w, *, k_block=0, shift=None, scale=None, emit_h=False, out_dtype=F32, name):
    n = x.shape[0]
    k, m = w.shape
    has_mod = shift is not None
    n_mod = shift.shape[0] if has_mod else 1
    per_mod = n // n_mod
    tm = _pick(per_mod, (1024, 512, 256, 128))
    tn = _pick(m, (512, 768, 256, 128))
    tiles_per_mod = per_mod // tm
    in_specs = [pl.BlockSpec((tm, k), lambda i, j: (i, k_block)), pl.BlockSpec((1, k), lambda i, j: (0, 0))]
    args = [x, g.reshape(1, k)]
    if has_mod:
        mod_spec = pl.BlockSpec((1, 1, k), lambda i, j: (i // tiles_per_mod, 0, 0))
        in_specs += [mod_spec, mod_spec]
        args += [shift, scale]
    in_specs.append(pl.BlockSpec((None, k, tn), lambda i, j: (j, 0, 0)))
    args.append(_col_tiles(w, tn))
    out_specs = pl.BlockSpec((tm, tn), lambda i, j: (i, j))
    out_shape = jax.ShapeDtypeStruct((n, m), out_dtype)
    if emit_h:
        out_specs = [out_specs, pl.BlockSpec((k, tm), lambda i, j: (0, i))]
        out_shape = [out_shape, jax.ShapeDtypeStruct((k, n), BF16)]
    return pl.pallas_call(
        functools.partial(_nmm_kernel, has_mod=has_mod, emit_h=emit_h),
        grid=(n // tm, m // tn),
        in_specs=in_specs,
        out_specs=out_specs,
        out_shape=out_shape,
        scratch_shapes=[pltpu.VMEM((tm, k), BF16)],
        compiler_params=_params("parallel", "arbitrary"),
        name=name,
    )(*args)


def _mm_res_kernel(a_ref, w_ref, res_ref, gate_ref, o_ref):
    acc = jnp.dot(a_ref[...], w_ref[...], preferred_element_type=F32)
    o_ref[...] = res_ref[...] + gate_ref[0] * acc


def _mm_res(a, w, res, gate, *, name):
    n, k = a.shape
    m = w.shape[1]
    per_mod = n // gate.shape[0]
    tm = _pick(per_mod, (1024, 512, 256, 128))
    tn = _pick(m, (512, 256, 128))
    tiles_per_mod = per_mod // tm
    return pl.pallas_call(
        _mm_res_kernel,
        grid=(n // tm, m // tn),
        in_specs=[
            pl.BlockSpec((tm, k), lambda i, j: (i, 0)),
            pl.BlockSpec((None, k, tn), lambda i, j: (j, 0, 0)),
            pl.BlockSpec((tm, tn), lambda i, j: (i, j)),
            pl.BlockSpec((1, 1, tn), lambda i, j: (i // tiles_per_mod, 0, j)),
        ],
        out_specs=pl.BlockSpec((tm, tn), lambda i, j: (i, j)),
        out_shape=jax.ShapeDtypeStruct((n, m), F32),
        compiler_params=_params("parallel", "parallel"),
        name=name,
    )(a, _col_tiles(w, tn), res, gate)


def _rope(y, cos, sin_up, sin_dn, n_freq):
    w = y.shape[-1]
    return y * cos + pltpu.roll(y, w - n_freq, 1) * sin_up + pltpu.roll(y, n_freq, 1) * sin_dn


def _sumsq(x):
    return jnp.sum(x * x, axis=-1, keepdims=True)


def _prep_aq_kernel(*refs, rope, scale):
    if rope:
        x_ref, gn_ref, gr_ref, cos_ref, su_ref, sd_ref, o_ref = refs
    else:
        x_ref, gn_ref, gr_ref, o_ref = refs
    gn, gr = gn_ref[...], gr_ref[...]
    for h in range(A_HEADS):
        xn = x_ref[:, h * 128:(h + 1) * 128]
        xr = x_ref[:, 1024 + h * 128:1024 + (h + 1) * 128]
        r = lax.rsqrt((_sumsq(xn) + _sumsq(xr)) * (1.0 / A_QK) + EPS)
        yn = xn * r * gn
        yr = xr * r * gr
        if rope:
            yr = _rope(yr, cos_ref[...], su_ref[...], sd_ref[...], A_ROPE // 4)
        o_ref[0, h, 0, :, 0:128] = (yn * scale).astype(BF16)
        o_ref[0, h, 0, :, 128:256] = (yr * scale).astype(BF16)


def _prep_akv_kernel(*refs, rope):
    if rope:
        x_ref, kr_ref, gn_ref, gr_ref, cos_ref, su_ref, sd_ref, k_ref, v_ref = refs
    else:
        x_ref, kr_ref, gn_ref, gr_ref, k_ref, v_ref = refs
    gn = gn_ref[...]
    kr = kr_ref[...]
    ss_r = _sumsq(kr)
    krg = kr * gr_ref[...]
    if rope:
        krg = _rope(krg, cos_ref[...], su_ref[...], sd_ref[...], A_ROPE // 4)
    for h in range(A_HEADS):
        xn = x_ref[:, h * 256:h * 256 + 128]
        r = lax.rsqrt((_sumsq(xn) + ss_r) * (1.0 / A_QK) + EPS)
        k_ref[0, h, :, 0:128] = (xn * r * gn).astype(BF16)
        k_ref[0, h, :, 128:256] = (krg * r).astype(BF16)
        v_ref[0, h] = x_ref[:, h * 256 + 128:(h + 1) * 256].astype(BF16)


def _norm64_pairs(x, g, lo):
    x2 = x * x
    s_lo = jnp.sum(jnp.where(lo, x2, 0.0), axis=-1, keepdims=True)
    s_hi = jnp.sum(jnp.where(lo, 0.0, x2), axis=-1, keepdims=True)
    r = jnp.where(lo, lax.rsqrt(s_lo * (1.0 / B_DIM) + EPS), lax.rsqrt(s_hi * (1.0 / B_DIM) + EPS))
    return x * r * g


def _prep_b_kernel(*refs, rope, scale):
    if rope:
        q_ref, k_ref, v_ref, gq_ref, gk_ref, cos_ref, su_ref, sd_ref, qo_ref, ko_ref, vo_ref = refs
    else:
        q_ref, k_ref, v_ref, gq_ref, gk_ref, qo_ref, ko_ref, vo_ref = refs
    lo = lax.broadcasted_iota(jnp.int32, (q_ref.shape[0], 2 * B_DIM), 1) < B_DIM
    for src, g_ref, dst, mul in ((q_ref, gq_ref, qo_ref, scale), (k_ref, gk_ref, ko_ref, None)):
        g = g_ref[...]
        for j in range(B_HEADS):
            y = _norm64_pairs(src[:, j * 128:(j + 1) * 128], g, lo)
            if rope:
                y = _rope(y, cos_ref[...], su_ref[...], sd_ref[...], B_DIM // 4)
            if mul is not None:
                y = y * mul
            yb = y.astype(BF16)
            if dst is qo_ref:
                dst[0, 2 * j, 0] = yb[:, 0:B_DIM]
                dst[0, 2 * j + 1, 0] = yb[:, B_DIM:2 * B_DIM]
            else:
                dst[0, 2 * j] = yb[:, 0:B_DIM]
                dst[0, 2 * j + 1] = yb[:, B_DIM:2 * B_DIM]
    for h in range(B_HEADS):
        vo_ref[0, h] = v_ref[:, h * B_V:(h + 1) * B_V].astype(BF16)


def _prep_c_kernel(*refs, rope, scale):
    if rope:
        q_ref, kv_ref, gq_ref, gk_ref, cos_ref, su_ref, sd_ref, qo_ref, ko_ref, vo_ref = refs
    else:
        q_ref, kv_ref, gq_ref, gk_ref, qo_ref, ko_ref, vo_ref = refs

    def norm_rope(x, g):
        y = x * lax.rsqrt(_sumsq(x) * (1.0 / C_DIM) + EPS) * g
        if rope:
            y = _rope(y, cos_ref[...], su_ref[...], sd_ref[...], C_DIM // 4)
        return y

    gq, gk = gq_ref[...], gk_ref[...]
    for h in range(C_HEADS):
        y = norm_rope(q_ref[:, h * C_DIM:(h + 1) * C_DIM], gq) * scale
        qo_ref[0, h // C_GROUP, h % C_GROUP] = y.astype(BF16)
    for h in range(C_KV_HEADS):
        ko_ref[0, h] = norm_rope(kv_ref[:, h * C_DIM:(h + 1) * C_DIM], gk).astype(BF16)
        vo_ref[0, h] = kv_ref[:, (C_KV_HEADS + h) * C_DIM:(C_KV_HEADS + h + 1) * C_DIM].astype(BF16)


def _prep_call(kernel, ins, in_blocks, consts, tables, outs, *, batch, seq, name):
    ts = _pick(seq, (256, 128))
    n_s = seq // ts
    in_specs = [pl.BlockSpec((ts, w), functools.partial(lambda b, i, c: (b * n_s + i, c), c=c)) for w, c in in_blocks]
    in_specs += [pl.BlockSpec(c.shape, lambda b, i: (0, 0)) for c in consts]
    if tables is not None:
        in_specs += [pl.BlockSpec((ts, LANE_V7X), lambda b, i: (i, 0)) for _ in tables]
    out_specs, out_shape = [], []
    for heads, d in outs:
        nz = len(heads)
        out_specs.append(pl.BlockSpec((1,) + heads + (ts, d),
                                      functools.partial(lambda b, i, nz: (b,) + (0,) * nz + (i, 0), nz=nz)))
        out_shape.append(jax.ShapeDtypeStruct((batch,) + heads + (seq, d), BF16))
    args = list(ins) + list(consts) + (list(tables) if tables is not None else [])
    return pl.pallas_call(
        kernel,
        grid=(batch, n_s),
        in_specs=in_specs,
        out_specs=out_specs,
        out_shape=out_shape,
        compiler_params=_params("parallel", "parallel"),
        name=name,
    )(*args)


def _flash_kernel(*refs, group, tq, n_src, tks):
    q_ref, kv_refs, o_ref = refs[0], refs[1:1 + 2 * n_src], refs[1 + 2 * n_src]
    m_scr, l_scr, acc_scr = refs[2 + 2 * n_src:]
    dv = LANE_V7X
    m_scr[...] = jnp.full(m_scr.shape, NEG_INF, F32)
    l_scr[...] = jnp.zeros(l_scr.shape, F32)
    acc_scr[...] = jnp.zeros(acc_scr.shape, F32)

    q = q_ref[0, 0].reshape(group * tq, q_ref.shape[-1])
    for src in range(n_src):
        k_ref, v_ref = kv_refs[2 * src], kv_refs[2 * src + 1]
        tk = tks[src]
        ones = jnp.ones((tk, LANE_V7X), BF16)
        for t in range(k_ref.shape[2] // tk):
            k = k_ref[0, 0, t * tk:(t + 1) * tk, :]
            v1 = jnp.concatenate([v_ref[0, 0, t * tk:(t + 1) * tk, :], ones], axis=1)
            s = lax.dot_general(q, k, (((1,), (1,)), ((), ())), preferred_element_type=F32)
            m_prev = m_scr[...]
            m_new = jnp.maximum(m_prev, jnp.max(s, axis=-1, keepdims=True))
            alpha = jnp.exp2(m_prev - m_new)
            p = jnp.exp2(s - jnp.concatenate([m_new] * (tk // LANE_V7X), axis=1))
            pv = jnp.dot(p.astype(BF16), v1, preferred_element_type=F32)
            acc_scr[...] = alpha * acc_scr[...] + pv[:, :dv]
            l_scr[...] = alpha * l_scr[...] + pv[:, dv:]
            m_scr[...] = m_new

    out = acc_scr[...] / l_scr[...]
    for g in range(group):
        o_ref[0, :, g * dv:(g + 1) * dv] = out[g * tq:(g + 1) * tq].astype(o_ref.dtype)


def _flash(q, kv_sources, *, out_dtype, name):
    b, hk, group, s, d = q.shape
    hv, dv = kv_sources[0][1].shape[1], kv_sources[0][1].shape[3]
    assert dv == LANE_V7X
    k_per_v = hk // hv
    tq = _pick(s, (FLASH_ROWS // group, 256 // group, 128 // group))
    tks = tuple(_pick(k.shape[2], (FLASH_TK, 256, LANE_V7X)) for k, _ in kv_sources)
    rows = group * tq
    stat = pltpu.VMEM((rows, LANE_V7X), F32)
    in_specs = [pl.BlockSpec((1, 1, group, tq, d), lambda bi, h, i: (bi, h, 0, i, 0))]
    args = [q]
    for k, v in kv_sources:
        t = k.shape[2]
        in_specs += [pl.BlockSpec((1, 1, t, d), lambda bi, h, i: (bi, h, 0, 0)),
                     pl.BlockSpec((1, 1, t, dv), lambda bi, h, i: (bi, h // k_per_v, 0, 0))]
        args += [k, v]
    return pl.pallas_call(
        functools.partial(_flash_kernel, group=group, tq=tq, n_src=len(kv_sources), tks=tks),
        grid=(b, hk, s // tq),
        in_specs=in_specs,
        out_specs=pl.BlockSpec((1, tq, group * dv), lambda bi, h, i: (bi, i, h)),
        out_shape=jax.ShapeDtypeStruct((b, s, hk * group * dv), out_dtype),
        scratch_shapes=[stat, stat, stat],
        compiler_params=_params("parallel", "parallel", "parallel"),
        name=name,
    )(*args)


def _merge_ab_kernel(oa_ref, ob_ref, g_ref, lv_ref, o_ref, *, lam_init):
    lv = lv_ref[...]
    lam = (jnp.exp(jnp.sum(lv[0:1] * lv[1:2], axis=-1, keepdims=True))
           - jnp.exp(jnp.sum(lv[2:3] * lv[3:4], axis=-1, keepdims=True)) + lam_init)
    n_a = A_HEADS * A_V
    o_ref[:, 0:n_a] = oa_ref[...].astype(BF16)
    g = g_ref[...]
    for h in range(B_HEADS):
        o = ob_ref[:, 2 * h * B_V:(2 * h + 1) * B_V] - lam * ob_ref[:, (2 * h + 1) * B_V:(2 * h + 2) * B_V]
        y = o * lax.rsqrt(_sumsq(o) * (1.0 / B_V) + EPS) * g * (1.0 - lam_init)
        o_ref[:, n_a + h * B_V:n_a + (h + 1) * B_V] = y.astype(BF16)


def _merge_ab(oa, ob, g_sub, lam_vec, lam_init):
    n = oa.shape[0]
    tm = _pick(n, (512, 256, 128))
    wa, wb = oa.shape[1], ob.shape[1]
    wo = wa + B_HEADS * B_V
    return pl.pallas_call(
        functools.partial(_merge_ab_kernel, lam_init=lam_init),
        grid=(n // tm,),
        in_specs=[
            pl.BlockSpec((tm, wa), lambda i: (i, 0)),
            pl.BlockSpec((tm, wb), lambda i: (i, 0)),
            pl.BlockSpec((1, B_V), lambda i: (0, 0)),
            pl.BlockSpec((4, B_DIM), lambda i: (0, 0)),
        ],
        out_specs=pl.BlockSpec((tm, wo), lambda i: (i, 0)),
        out_shape=jax.ShapeDtypeStruct((n, wo), BF16),
        compiler_params=_params("parallel"),
        name="merge_ab",
    )(oa, ob, g_sub.reshape(1, B_V), lam_vec)


def _top_rows(s, k):
    vals = []
    cur = s
    for _ in range(k):
        mx = jnp.max(cur, axis=0, keepdims=True)
        vals.append(mx)
        cur = jnp.where(cur == mx, NEG_INF, cur)
    return jnp.concatenate(vals, axis=0)


def _sort_desc(a, bitonic):
    n = len(a)
    a = list(a)
    for k in ((n,) if bitonic else [2 << e for e in range(n.bit_length() - 1)]):
        j = k // 2
        while j >= 1:
            for i in range(n):
                l = i ^ j
                if l > i:
                    hi, lo = jnp.maximum(a[i], a[l]), jnp.minimum(a[i], a[l])
                    a[i], a[l] = (hi, lo) if (i & k) == 0 or bitonic else (lo, hi)
            j //= 2
    return a


def _top16_of_128(s):
    n = P_TOPK
    a = _sort_desc([s[8 * i:8 * i + 8] for i in range(n)], False)
    for shift in (4, 2, 1):
        a = _sort_desc([jnp.maximum(a[i], pltpu.roll(a[n - 1 - i], shift, 0)) for i in range(n)], True)
    return jnp.concatenate([x[0:1] for x in a], axis=0)


def _peer_stats_kernel(q_ref, keys_ref, c1_ref, e2_ref, th_ref):
    dn = (((1,), (1,)), ((), ()))
    s1 = lax.dot_general(keys_ref[0], q_ref[:, 0:N_KEYS], dn, preferred_element_type=F32)
    s2 = lax.dot_general(keys_ref[1], q_ref[:, N_KEYS:2 * N_KEYS], dn, preferred_element_type=F32)
    v1 = _top16_of_128(s1)
    v2 = _top16_of_128(s2)
    n_b = [P_TOPK // (a + 1) for a in range(P_TOPK)]
    cand = jnp.concatenate([v1[a:a + 1] + v2[0:n_b[a]] for a in range(P_TOPK)], axis=0)
    best = _top_rows(cand, P_TOPK)
    inv_z = 1.0 / jnp.sum(jnp.exp(best - best[0:1]), axis=0, keepdims=True)
    c1_ref[0] = jnp.exp(s1 - v1[0:1]) * inv_z
    e2_ref[0] = jnp.exp(s2 - v2[0:1])
    c1v = jnp.exp(v1 - v1[0:1]) * inv_z
    e2v = jnp.exp(v2 - v2[0:1])
    gates = jnp.concatenate([c1v[a:a + 1] * e2v[0:n_b[a]] for a in range(P_TOPK)], axis=0)
    th_ref[0] = jnp.min(jnp.where(cand >= best[P_TOPK - 1:P_TOPK], gates, jnp.inf), axis=0, keepdims=True)


def _peer_stats(q, keys):
    n = q.shape[0]
    tm = _pick(n, (256, 128))
    big = pl.BlockSpec((1, N_KEYS, tm), lambda i, h: (h, 0, i))
    big_shape = jax.ShapeDtypeStruct((P_HEADS, N_KEYS, n), F32)
    return pl.pallas_call(
        _peer_stats_kernel,
        grid=(n // tm, P_HEADS),
        in_specs=[
            pl.BlockSpec((tm, P_DQ), lambda i, h: (i, h)),
            pl.BlockSpec((2, N_KEYS, P_DQ // 2), lambda i, h: (h, 0, 0)),
        ],
        out_specs=[big, big, pl.BlockSpec((1, 1, tm), lambda i, h: (h, 0, i))],
        out_shape=[big_shape, big_shape, jax.ShapeDtypeStruct((P_HEADS, 1, n), F32)],
        compiler_params=_params("parallel", "parallel"),
        name="peer_stats",
    )(q, keys)


_GELU_C1 = -2.0 * math.sqrt(2.0 / math.pi) * LOG2_E
_GELU_C3 = 0.044715 * _GELU_C1


def _gated_gelu_tanh(m, a):
    e = jnp.exp2(a * (_GELU_C3 * (a * a) + _GELU_C1))
    return (m * a) / (1.0 + e)


def _peer_dense_kernel(h_ref, u_ref, vt_ref, c1_ref, e2_ref, th_ref, res_ref, gate_ref,
                       o_ref, acc_ref, at_ref, wt_ref, *, te, tm, tc):
    j = pl.program_id(1)

    @pl.when(j == 0)
    def _():
        acc_ref[...] = jnp.zeros(acc_ref.shape, F32)

    at_ref[...] = jnp.dot(u_ref[...], h_ref[...], preferred_element_type=F32)
    for i1 in range(te // N_KEYS):
        rows = slice(i1 * N_KEYS, (i1 + 1) * N_KEYS)
        for c in range(tm // tc):
            cols = slice(c * tc, (c + 1) * tc)
            m = jnp.zeros((N_KEYS, tc), F32)
            for h in range(P_HEADS):
                g = e2_ref[h, :, cols] * c1_ref[h, i1:i1 + 1, cols]
                m = m + jnp.where(g >= th_ref[h, :, cols], g, 0.0)
            wt_ref[rows, cols] = _gated_gelu_tanh(m, at_ref[rows, cols]).astype(BF16)
    acc_ref[...] += jnp.dot(vt_ref[...], wt_ref[...], preferred_element_type=F32)

    @pl.when(j == pl.num_programs(1) - 1)
    def _():
        o_ref[...] = res_ref[...] + gate_ref[0] * acc_ref[...].T


def _peer_dense(h, u, vt, layer, stats, res, gate):
    d, n = h.shape
    e = u.shape[1]
    c1, e2, theta = stats
    per_mod = n // gate.shape[0]
    tm = _pick(per_mod, (512, 256, 128))
    te = PEER_TE
    tc = _pick(tm, (256, 128))
    tiles_per_mod = per_mod // tm
    once = dict(pipeline_mode=pl.Buffered(1))
    big = pl.BlockSpec((P_HEADS, N_KEYS, tm), lambda i, j: (0, 0, i), **once)
    row = pl.BlockSpec((P_HEADS, te // N_KEYS, tm), lambda i, j: (0, j, i))
    return pl.pallas_call(
        functools.partial(_peer_dense_kernel, te=te, tm=tm, tc=tc),
        grid=(n // tm, e // te),
        in_specs=[
            pl.BlockSpec((d, tm), lambda i, j: (0, i)),
            pl.BlockSpec((None, te, d), lambda i, j: (layer, j, 0)),
            pl.BlockSpec((None, None, d, te), lambda i, j: (layer, j, 0, 0)),
            row, big,
            pl.BlockSpec((P_HEADS, 1, tm), lambda i, j: (0, 0, i)),
            pl.BlockSpec((tm, d), lambda i, j: (i, 0), **once),
            pl.BlockSpec((1, 1, d), lambda i, j: (i // tiles_per_mod, 0, 0)),
        ],
        out_specs=pl.BlockSpec((tm, d), lambda i, j: (i, 0), **once),
        out_shape=jax.ShapeDtypeStruct((n, d), F32),
        scratch_shapes=[pltpu.VMEM((d, tm), F32), pltpu.VMEM((te, tm), F32), pltpu.VMEM((te, tm), BF16)],
        compiler_params=_params("parallel", "arbitrary"),
        name="peer_dense",
    )(h, u, vt, c1, e2, theta, res, gate)


def _rope_tables(seq, rot_dim):
    n_freq = rot_dim // 4
    freqs = ROPE_BASE ** (-jnp.arange(n_freq, dtype=F32) / n_freq)
    pos = jnp.arange(seq)
    row = (pos // GRID_W).astype(F32)[:, None] * freqs
    col = (pos % GRID_W).astype(F32)[:, None] * freqs
    zero = jnp.zeros_like(row)
    cos = jnp.concatenate([jnp.cos(row), jnp.cos(row), jnp.cos(col), jnp.cos(col)], axis=-1)
    s_up = jnp.concatenate([-jnp.sin(row), zero, -jnp.sin(col), zero], axis=-1)
    s_dn = jnp.concatenate([zero, jnp.sin(row), zero, jnp.sin(col)], axis=-1)
    return cos, s_up, s_dn


def _tile_tables(tabs, reps, width):
    out = []
    for idx, t in enumerate(tabs):
        t = jnp.tile(t, (1, reps))
        pad = width - t.shape[1]
        if pad:
            fill = jnp.ones if idx == 0 else jnp.zeros
            t = jnp.concatenate([t, fill((t.shape[0], pad), F32)], axis=-1)
        out.append(t)
    return out


def _pad_lanes(v, width):
    return jnp.pad(v, (0, width - v.shape[0])).reshape(1, width)


def _layout_w_in_ab(w):
    d = w.shape[0]
    cq, ckv, kr, bq, bk, bv = jnp.split(w, [768, 1280, 1344, 2368, 3392], axis=1)
    return jnp.concatenate([cq, kr, jnp.zeros((d, AB_BQ - AB_KR - A_ROPE), w.dtype), bq, bk, bv, ckv],
                           axis=1).astype(BF16)


def _layout_w_uq(w):
    k = w.shape[0]
    w3 = w.reshape(k, A_HEADS, A_QK)
    nope = w3[:, :, :A_NOPE].reshape(k, A_HEADS * A_NOPE)
    rope = jnp.pad(w3[:, :, A_NOPE:], ((0, 0), (0, 0), (0, LANE_V7X - A_ROPE))).reshape(k, A_HEADS * LANE_V7X)
    return jnp.concatenate([nope, rope], axis=1).astype(BF16)


def _mixer_ab(tok, mod, p, tables, *, batch, seq, rope):
    proj = _nmm(tok, p["g_norm1"], p["w_in"], shift=mod[0], scale=mod[1], name="ab_in")
    qa_raw = _nmm(proj, p["g_cq"], p["w_uq"], k_block=AB_CQ // A_Q_LORA, name="ab_uq")
    kva_raw = _nmm(proj, p["g_ckv"], p["w_ukv"], k_block=AB_CKV // A_KV_LORA, name="ab_ukv")
    sa, sb = A_QK ** -0.5 * LOG2_E, B_DIM ** -0.5 * LOG2_E
    (qa,) = _prep_call(
        functools.partial(_prep_aq_kernel, rope=rope, scale=sa), [qa_raw], [(qa_raw.shape[1], 0)],
        [p["gqa_n"], p["gqa_r"]], tables["a"] if rope else None, [((A_HEADS, 1), 2 * LANE_V7X)],
        batch=batch, seq=seq, name="prep_aq")
    ka, va = _prep_call(
        functools.partial(_prep_akv_kernel, rope=rope), [kva_raw, proj],
        [(kva_raw.shape[1], 0), (LANE_V7X, AB_KR // LANE_V7X)],
        [p["gka_n"], p["gka_r"]], tables["a"] if rope else None,
        [((A_HEADS,), 2 * LANE_V7X), ((A_HEADS,), A_V)], batch=batch, seq=seq, name="prep_akv")
    wb = B_HEADS * 2 * B_DIM
    qb, kb, vb = _prep_call(
        functools.partial(_prep_b_kernel, rope=rope, scale=sb), [proj, proj, proj],
        [(wb, AB_BQ // wb), (wb, AB_BK // wb), (wb, AB_BV // wb)],
        [p["gqb"], p["gkb"]], tables["b"] if rope else None,
        [((2 * B_HEADS, 1), B_DIM), ((2 * B_HEADS,), B_DIM), ((B_HEADS,), B_V)],
        batch=batch, seq=seq, name="prep_b")
    return (qa, ka, va), (qb, kb, vb)


def _mixer_c(tok, mod, p, tables, *, batch, seq, rope):
    proj = _nmm(tok, p["g_norm1"], p["w_in"], shift=mod[0], scale=mod[1], name="c_in")
    wq = C_HEADS * C_DIM
    wkv = 2 * C_KV_HEADS * C_DIM
    return _prep_call(
        functools.partial(_prep_c_kernel, rope=rope, scale=C_DIM ** -0.5 * LOG2_E), [proj, proj],
        [(wq, 0), (wkv, wq // wkv)], [p["gq"], p["gk"]], tables["c"] if rope else None,
        [((C_KV_HEADS, C_GROUP), C_DIM), ((C_KV_HEADS,), C_DIM), ((C_KV_HEADS,), C_DIM)],
        batch=batch, seq=seq, name="prep_c")


def _peer(tok, mod, gate, p, name):
    q, h = _nmm(tok, p["g_norm2"], p["w_pq"], shift=mod[0], scale=mod[1], emit_h=True, out_dtype=BF16, name=name)
    stats = _peer_stats(q, p["keys"])
    return _peer_dense(h, p["u_all"], p["vt_all"], p["layer"], stats, tok, gate)


def kernel(x, c, ctx, c_ctx, w_mod, b_mod, g_norm1, g_norm2, w_in_ab, g_cq, w_uq, g_ckv, w_ukv, g_qn_a, g_kn_a,
           lam_vec, g_qn_b, g_kn_b, g_sub_b, w_out_ab, w_in_c, g_qn_c, g_kn_c, w_out_c, w_pq, sub_keys,
           expert_u, expert_v):
    batch, seq, d = x.shape
    ctx_len = ctx.shape[1]
    depth = w_mod.shape[0]
    assert batch + 1 <= 8

    cz = jnp.concatenate([c, c_ctx[None, :], jnp.zeros((8 - batch - 1, d), F32)], axis=0)
    mod_all = _modulation(cz, w_mod, b_mod).reshape(depth, 8, 6, d)

    tab64 = _rope_tables(seq, A_ROPE)
    tables = {
        "a": _tile_tables(tab64, 1, LANE_V7X),
        "b": _tile_tables(tab64, 2, LANE_V7X),
        "c": _rope_tables(seq, C_DIM),
    }

    xt = x.reshape(batch * seq, d)
    zt = ctx.reshape(batch * ctx_len, d)
    u_all = expert_u.astype(BF16)
    n_exp = expert_v.shape[1]
    vt_all = jnp.swapaxes(expert_v.reshape(depth, n_exp // PEER_TE, PEER_TE, d), 2, 3).astype(BF16)

    for layer in range(depth):
        last = layer == depth - 1
        e = layer // 2
        mx = [mod_all[layer, :batch, i][:, None, :] for i in range(6)]
        mz = [mod_all[layer, batch:batch + 1, i][:, None, :] for i in range(6)]
        pp = {
            "g_norm2": g_norm2[layer], "w_pq": w_pq[layer].astype(BF16),
            "keys": sub_keys[layer].reshape(2 * P_HEADS, N_KEYS, P_DQ // 2).astype(BF16),
            "u_all": u_all, "vt_all": vt_all, "layer": layer,
        }
        if layer % 2 == 0:
            lam_init = 0.8 - 0.6 * math.exp(-0.3 * layer)
            p = {
                "g_norm1": g_norm1[layer], "w_in": _layout_w_in_ab(w_in_ab[e]),
                "g_cq": g_cq[e], "w_uq": _layout_w_uq(w_uq[e]),
                "g_ckv": g_ckv[e], "w_ukv": w_ukv[e].astype(BF16),
                "gqa_n": g_qn_a[e][:A_NOPE].reshape(1, A_NOPE), "gqa_r": _pad_lanes(g_qn_a[e][A_NOPE:], LANE_V7X),
                "gka_n": g_kn_a[e][:A_NOPE].reshape(1, A_NOPE), "gka_r": _pad_lanes(g_kn_a[e][A_NOPE:], LANE_V7X),
                "gqb": jnp.tile(g_qn_b[e], 2).reshape(1, 2 * B_DIM), "gkb": jnp.tile(g_kn_b[e], 2).reshape(1, 2 * B_DIM),
            }
            w_out = w_out_ab[e].astype(BF16)
            (qa, ka, va), (qb, kb, vb) = _mixer_ab(xt, mx, p, tables, batch=batch, seq=seq, rope=True)
            (qaz, kaz, vaz), (qbz, kbz, vbz) = _mixer_ab(zt, mz, p, tables, batch=batch, seq=ctx_len, rope=False)
            oa = _flash(qa, [(ka, va), (kaz, vaz)], out_dtype=BF16, name="flash_a")
            ob = _flash(qb, [(kb, vb), (kbz, vbz)], out_dtype=F32, name="flash_b")
            att = _merge_ab(oa.reshape(batch * seq, -1), ob.reshape(batch * seq, -1), g_sub_b[e], lam_vec[e], lam_init)
            if not last:
                oaz = _flash(qaz, [(kaz, vaz)], out_dtype=BF16, name="flash_az")
                obz = _flash(qbz, [(kbz, vbz)], out_dtype=F32, name="flash_bz")
                attz = _merge_ab(oaz.reshape(batch * ctx_len, -1), obz.reshape(batch * ctx_len, -1),
                                 g_sub_b[e], lam_vec[e], lam_init)
        else:
            p = {
                "g_norm1": g_norm1[layer], "w_in": w_in_c[e].astype(BF16),
                "gq": g_qn_c[e].reshape(1, C_DIM), "gk": g_kn_c[e].reshape(1, C_DIM),
            }
            w_out = w_out_c[e].astype(BF16)
            q, k, v = _mixer_c(xt, mx, p, tables, batch=batch, seq=seq, rope=True)
            qz, kz, vz = _mixer_c(zt, mz, p, tables, batch=batch, seq=ctx_len, rope=False)
            att = _flash(q, [(k, v), (kz, vz)], out_dtype=BF16, name="flash_c").reshape(batch * seq, -1)
            if not last:
                attz = _flash(qz, [(kz, vz)], out_dtype=BF16, name="flash_cz").reshape(batch * ctx_len, -1)

        xt = _mm_res(att, w_out, xt, mx[2], name="out_proj")
        xt = _peer(xt, (mx[3], mx[4]), mx[5], pp, "peer_q")
        if not last:
            zt = _mm_res(attz, w_out, zt, mz[2], name="out_proj_z")
            zt = _peer(zt, (mz[3], mz[4]), mz[5], pp, "peer_qz")
    return xt.reshape(batch, seq, d)
```

```python
import functools
import math

import jax
import jax.numpy as jnp
from jax import lax
from jax.experimental import pallas as pl
from jax.experimental.pallas import tpu as pltpu

F32 = jnp.float32
BF16 = jnp.bfloat16

LANE_V7X = 128
VMEM_LIMIT_V7X = 56 * 1024 * 1024

GRID_W = 64
ROPE_BASE = 10000.0
EPS = 1e-6
A_HEADS, A_Q_LORA, A_KV_LORA, A_NOPE, A_ROPE, A_V = 8, 768, 512, 128, 64, 128
A_QK = A_NOPE + A_ROPE
B_HEADS, B_DIM, B_V = 8, 64, 128
C_HEADS, C_KV_HEADS, C_DIM = 16, 4, 128
C_GROUP = C_HEADS // C_KV_HEADS
P_HEADS, N_KEYS, P_DQ, P_TOPK = 8, 128, 256, 16
PEER_TE = 1024
FLASH_TK = 256
FLASH_ROWS = 2048
NEG_INF = float("-inf")
LOG2_E = math.log2(math.e)

AB_CQ, AB_KR, AB_BQ, AB_BK, AB_BV, AB_CKV = 0, 768, 1024, 2048, 3072, 4096


def _pick(n, prefs):
    for p in prefs:
        if n % p == 0:
            return p
    return n


def _col_tiles(w, tn):
    k, m = w.shape
    return w.reshape(k, m // tn, tn).transpose(1, 0, 2)


def _params(*sem):
    return pltpu.CompilerParams(dimension_semantics=sem, vmem_limit_bytes=VMEM_LIMIT_V7X)


def _mod_kernel(c_ref, w_ref, b_ref, o_ref):
    c = c_ref[...]
    s = c * (1.0 / (1.0 + jnp.exp(-c)))
    o_ref[0] = jnp.dot(s.astype(BF16), w_ref[0].astype(BF16), preferred_element_type=F32) + b_ref[0]


def _modulation(cz, w_mod, b_mod):
    depth, d, n = w_mod.shape
    tn = _pick(n, (1024, 768, 512, 256, 128))
    return pl.pallas_call(
        _mod_kernel,
        grid=(depth, n // tn),
        in_specs=[
            pl.BlockSpec((8, d), lambda l, j: (0, 0)),
            pl.BlockSpec((1, d, tn), lambda l, j: (l, 0, j)),
            pl.BlockSpec((1, 1, tn), lambda l, j: (l, 0, j)),
        ],
        out_specs=pl.BlockSpec((1, 8, tn), lambda l, j: (l, 0, j)),
        out_shape=jax.ShapeDtypeStruct((depth, 8, n), F32),
        compiler_params=_params("parallel", "parallel"),
        name="modulation",
    )(cz, w_mod, b_mod.reshape(depth, 1, n))


def _nmm_kernel(*refs, has_mod, emit_h):
    if has_mod:
        x_ref, g_ref, sh_ref, sc_ref, w_ref = refs[:5]
        rest = refs[5:]
    else:
        x_ref, g_ref, w_ref = refs[:3]
        rest = refs[3:]
    if emit_h:
        o_ref, ho_ref, h_scr = rest
    else:
        o_ref, h_scr = rest

    @pl.when(pl.program_id(1) == 0)
    def _():
        x = x_ref[...].astype(F32)
        y = x * lax.rsqrt(jnp.mean(x * x, axis=-1, keepdims=True) + EPS) * g_ref[...]
        if has_mod:
            y = y * (1.0 + sc_ref[0]) + sh_ref[0]
        h_scr[...] = y.astype(BF16)
        if emit_h:
            ho_ref[...] = y.T.astype(BF16)

    o_ref[...] = jnp.dot(h_scr[...], w_ref[...], preferred_element_type=F32).astype(o_ref.dtype)


def _nmm(x, g, w, *, k_block=0, shift=None, scale=None, emit_h=False, out_dtype=F32, name):
    n = x.shape[0]
    k, m = w.shape
    has_mod = shift is not None
    n_mod = shift.shape[0] if has_mod else 1
    per_mod = n // n_mod
    tm = _pick(per_mod, (1024, 512, 256, 128))
    tn = _pick(m, (512, 768, 256, 128))
    tiles_per_mod = per_mod // tm
    in_specs = [pl.BlockSpec((tm, k), lambda i, j: (i, k_block)), pl.BlockSpec((1, k), lambda i, j: (0, 0))]
    args = [x, g.reshape(1, k)]
    if has_mod:
        mod_spec = pl.BlockSpec((1, 1, k), lambda i, j: (i // tiles_per_mod, 0, 0))
        in_specs += [mod_spec, mod_spec]
        args += [shift, scale]
    in_specs.append(pl.BlockSpec((None, k, tn), lambda i, j: (j, 0, 0)))
    args.append(_col_tiles(w, tn))
    out_specs = pl.BlockSpec((tm, tn), lambda i, j: (i, j))
    out_shape = jax.ShapeDtypeStruct((n, m), out_dtype)
    if emit_h:
        out_specs = [out_specs, pl.BlockSpec((k, tm), lambda i, j: (0, i))]
        out_shape = [out_shape, jax.ShapeDtypeStruct((k, n), BF16)]
    return pl.pallas_call(
        functools.partial(_nmm_kernel, has_mod=has_mod, emit_h=emit_h),
        grid=(n // tm, m // tn),
        in_specs=in_specs,
        out_specs=out_specs,
        out_shape=out_shape,
        scratch_shapes=[pltpu.VMEM((tm, k), BF16)],
        compiler_params=_params("parallel", "arbitrary"),
        name=name,
    )(*args)


def _mm_res_kernel(a_ref, w_ref, res_ref, gate_ref, o_ref):
    acc = jnp.dot(a_ref[...], w_ref[...], preferred_element_type=F32)
    o_ref[...] = res_ref[...] + gate_ref[0] * acc


def _mm_res(a, w, res, gate, *, name):
    n, k = a.shape
    m = w.shape[1]
    per_mod = n // gate.shape[0]
    tm = _pick(per_mod, (1024, 512, 256, 128))
    tn = _pick(m, (512, 256, 128))
    tiles_per_mod = per_mod // tm
    return pl.pallas_call(
        _mm_res_kernel,
        grid=(n // tm, m // tn),
        in_specs=[
            pl.BlockSpec((tm, k), lambda i, j: (i, 0)),
            pl.BlockSpec((None, k, tn), lambda i, j: (j, 0, 0)),
            pl.BlockSpec((tm, tn), lambda i, j: (i, j)),
            pl.BlockSpec((1, 1, tn), lambda i, j: (i // tiles_per_mod, 0, j)),
        ],
        out_specs=pl.BlockSpec((tm, tn), lambda i, j: (i, j)),
        out_shape=jax.ShapeDtypeStruct((n, m), F32),
        compiler_params=_params("parallel", "parallel"),
        name=name,
    )(a, _col_tiles(w, tn), res, gate)


def _rope(y, cos, sin_up, sin_dn, n_freq):
    w = y.shape[-1]
    return y * cos + pltpu.roll(y, w - n_freq, 1) * sin_up + pltpu.roll(y, n_freq, 1) * sin_dn


def _sumsq(x):
    return jnp.sum(x * x, axis=-1, keepdims=True)


def _prep_aq_kernel(*refs, rope, scale):
    if rope:
        x_ref, gn_ref, gr_ref, cos_ref, su_ref, sd_ref, o_ref = refs
    else:
        x_ref, gn_ref, gr_ref, o_ref = refs
    gn, gr = gn_ref[...], gr_ref[...]
    for h in range(A_HEADS):
        xn = x_ref[:, h * 128:(h + 1) * 128]
        xr = x_ref[:, 1024 + h * 128:1024 + (h + 1) * 128]
        r = lax.rsqrt((_sumsq(xn) + _sumsq(xr)) * (1.0 / A_QK) + EPS)
        yn = xn * r * gn
        yr = xr * r * gr
        if rope:
            yr = _rope(yr, cos_ref[...], su_ref[...], sd_ref[...], A_ROPE // 4)
        o_ref[0, h, 0, :, 0:128] = (yn * scale).astype(BF16)
        o_ref[0, h, 0, :, 128:256] = (yr * scale).astype(BF16)


def _prep_akv_kernel(*refs, rope):
    if rope:
        x_ref, kr_ref, gn_ref, gr_ref, cos_ref, su_ref, sd_ref, k_ref, v_ref = refs
    else:
        x_ref, kr_ref, gn_ref, gr_ref, k_ref, v_ref = refs
    gn = gn_ref[...]
    kr = kr_ref[...]
    ss_r = _sumsq(kr)
    krg = kr * gr_ref[...]
    if rope:
        krg = _rope(krg, cos_ref[...], su_ref[...], sd_ref[...], A_ROPE // 4)
    for h in range(A_HEADS):
        xn = x_ref[:, h * 256:h * 256 + 128]
        r = lax.rsqrt((_sumsq(xn) + ss_r) * (1.0 / A_QK) + EPS)
        k_ref[0, h, :, 0:128] = (xn * r * gn).astype(BF16)
        k_ref[0, h, :, 128:256] = (krg * r).astype(BF16)
        v_ref[0, h] = x_ref[:, h * 256 + 128:(h + 1) * 256].astype(BF16)


def _norm64_pairs(x, g, lo):
    x2 = x * x
    s_lo = jnp.sum(jnp.where(lo, x2, 0.0), axis=-1, keepdims=True)
    s_hi = jnp.sum(jnp.where(lo, 0.0, x2), axis=-1, keepdims=True)
    r = jnp.where(lo, lax.rsqrt(s_lo * (1.0 / B_DIM) + EPS), lax.rsqrt(s_hi * (1.0 / B_DIM) + EPS))
    return x * r * g


def _prep_b_kernel(*refs, rope, scale):
    if rope:
        q_ref, k_ref, v_ref, gq_ref, gk_ref, cos_ref, su_ref, sd_ref, qo_ref, ko_ref, vo_ref = refs
    else:
        q_ref, k_ref, v_ref, gq_ref, gk_ref, qo_ref, ko_ref, vo_ref = refs
    lo = lax.broadcasted_iota(jnp.int32, (q_ref.shape[0], 2 * B_DIM), 1) < B_DIM
    for src, g_ref, dst, mul in ((q_ref, gq_ref, qo_ref, scale), (k_ref, gk_ref, ko_ref, None)):
        g = g_ref[...]
        for j in range(B_HEADS):
            y = _norm64_pairs(src[:, j * 128:(j + 1) * 128], g, lo)
            if rope:
                y = _rope(y, cos_ref[...], su_ref[...], sd_ref[...], B_DIM // 4)
            if mul is not None:
                y = y * mul
            yb = y.astype(BF16)
            if dst is qo_ref:
                dst[0, 2 * j, 0] = yb[:, 0:B_DIM]
                dst[0, 2 * j + 1, 0] = yb[:, B_DIM:2 * B_DIM]
            else:
                dst[0, 2 * j] = yb[:, 0:B_DIM]
                dst[0, 2 * j + 1] = yb[:, B_DIM:2 * B_DIM]
    for h in range(B_HEADS):
        vo_ref[0, h] = v_ref[:, h * B_V:(h + 1) * B_V].astype(BF16)


def _prep_c_kernel(*refs, rope, scale):
    if rope:
        q_ref, kv_ref, gq_ref, gk_ref, cos_ref, su_ref, sd_ref, qo_ref, ko_ref, vo_ref = refs
    else:
        q_ref, kv_ref, gq_ref, gk_ref, qo_ref, ko_ref, vo_ref = refs

    def norm_rope(x, g):
        y = x * lax.rsqrt(_sumsq(x) * (1.0 / C_DIM) + EPS) * g
        if rope:
            y = _rope(y, cos_ref[...], su_ref[...], sd_ref[...], C_DIM // 4)
        return y

    gq, gk = gq_ref[...], gk_ref[...]
    for h in range(C_HEADS):
        y = norm_rope(q_ref[:, h * C_DIM:(h + 1) * C_DIM], gq) * scale
        qo_ref[0, h // C_GROUP, h % C_GROUP] = y.astype(BF16)
    for h in range(C_KV_HEADS):
        ko_ref[0, h] = norm_rope(kv_ref[:, h * C_DIM:(h + 1) * C_DIM], gk).astype(BF16)
        vo_ref[0, h] = kv_ref[:, (C_KV_HEADS + h) * C_DIM:(C_KV_HEADS + h + 1) * C_DIM].astype(BF16)


def _prep_call(kernel, ins, in_blocks, consts, tables, outs, *, batch, seq, name):
    ts = _pick(seq, (256, 128))
    n_s = seq // ts
    in_specs = [pl.BlockSpec((ts, w), functools.partial(lambda b, i, c: (b * n_s + i, c), c=c)) for w, c in in_blocks]
    in_specs += [pl.BlockSpec(c.shape, lambda b, i: (0, 0)) for c in consts]
    if tables is not None:
        in_specs += [pl.BlockSpec((ts, LANE_V7X), lambda b, i: (i, 0)) for _ in tables]
    out_specs, out_shape = [], []
    for heads, d in outs:
        nz = len(heads)
        out_specs.append(pl.BlockSpec((1,) + heads + (ts, d),
                                      functools.partial(lambda b, i, nz: (b,) + (0,) * nz + (i, 0), nz=nz)))
        out_shape.append(jax.ShapeDtypeStruct((batch,) + heads + (seq, d), BF16))
    args = list(ins) + list(consts) + (list(tables) if tables is not None else [])
    return pl.pallas_call(
        kernel,
        grid=(batch, n_s),
        in_specs=in_specs,
        out_specs=out_specs,
        out_shape=out_shape,
        compiler_params=_params("parallel", "parallel"),
        name=name,
    )(*args)


def _flash_kernel(*refs, group, tq, n_src, tks):
    q_ref, kv_refs, o_ref = refs[0], refs[1:1 + 2 * n_src], refs[1 + 2 * n_src]
    m_scr, l_scr, acc_scr = refs[2 + 2 * n_src:]
    dv = LANE_V7X
    m_scr[...] = jnp.full(m_scr.shape, NEG_INF, F32)
    l_scr[...] = jnp.zeros(l_scr.shape, F32)
    acc_scr[...] = jnp.zeros(acc_scr.shape, F32)

    q = q_ref[0, 0].reshape(group * tq, q_ref.shape[-1])
    for src in range(n_src):
        k_ref, v_ref = kv_refs[2 * src], kv_refs[2 * src + 1]
        tk = tks[src]
        ones = jnp.ones((tk, LANE_V7X), BF16)
        for t in range(k_ref.shape[2] // tk):
            k = k_ref[0, 0, t * tk:(t + 1) * tk, :]
            v1 = jnp.concatenate([v_ref[0, 0, t * tk:(t + 1) * tk, :], ones], axis=1)
            s = lax.dot_general(q, k, (((1,), (1,)), ((), ())), preferred_element_type=F32)
            m_prev = m_scr[...]
            m_new = jnp.maximum(m_prev, jnp.max(s, axis=-1, keepdims=True))
            alpha = jnp.exp2(m_prev - m_new)
            p = jnp.exp2(s - jnp.concatenate([m_new] * (tk // LANE_V7X), axis=1))
            pv = jnp.dot(p.astype(BF16), v1, preferred_element_type=F32)
            acc_scr[...] = alpha * acc_scr[...] + pv[:, :dv]
            l_scr[...] = alpha * l_scr[...] + pv[:, dv:]
            m_scr[...] = m_new

    out = acc_scr[...] / l_scr[...]
    for g in range(group):
        o_ref[0, :, g * dv:(g + 1) * dv] = out[g * tq:(g + 1) * tq].astype(o_ref.dtype)


def _flash(q, kv_sources, *, out_dtype, name):
    b, hk, group, s, d = q.shape
    hv, dv = kv_sources[0][1].shape[1], kv_sources[0][1].shape[3]
    assert dv == LANE_V7X
    k_per_v = hk // hv
    tq = _pick(s, (FLASH_ROWS // group, 256 // group, 128 // group))
    tks = tuple(_pick(k.shape[2], (FLASH_TK, 256, LANE_V7X)) for k, _ in kv_sources)
    rows = group * tq
    stat = pltpu.VMEM((rows, LANE_V7X), F32)
    in_specs = [pl.BlockSpec((1, 1, group, tq, d), lambda bi, h, i: (bi, h, 0, i, 0))]
    args = [q]
    for k, v in kv_sources:
        t = k.shape[2]
        in_specs += [pl.BlockSpec((1, 1, t, d), lambda bi, h, i: (bi, h, 0, 0)),
                     pl.BlockSpec((1, 1, t, dv), lambda bi, h, i: (bi, h // k_per_v, 0, 0))]
        args += [k, v]
    return pl.pallas_call(
        functools.partial(_flash_kernel, group=group, tq=tq, n_src=len(kv_sources), tks=tks),
        grid=(b, hk, s // tq),
        in_specs=in_specs,
        out_specs=pl.BlockSpec((1, tq, group * dv), lambda bi, h, i: (bi, i, h)),
        out_shape=jax.ShapeDtypeStruct((b, s, hk * group * dv), out_dtype),
        scratch_shapes=[stat, stat, stat],
        compiler_params=_params("parallel", "parallel", "parallel"),
        name=name,
    )(*args)


def _merge_ab_kernel(oa_ref, ob_ref, g_ref, lv_ref, o_ref, *, lam_init):
    lv = lv_ref[...]
    lam = (jnp.exp(jnp.sum(lv[0:1] * lv[1:2], axis=-1, keepdims=True))
           - jnp.exp(jnp.sum(lv[2:3] * lv[3:4], axis=-1, keepdims=True)) + lam_init)
    n_a = A_HEADS * A_V
    o_ref[:, 0:n_a] = oa_ref[...].astype(BF16)
    g = g_ref[...]
    for h in range(B_HEADS):
        o = ob_ref[:, 2 * h * B_V:(2 * h + 1) * B_V] - lam * ob_ref[:, (2 * h + 1) * B_V:(2 * h + 2) * B_V]
        y = o * lax.rsqrt(_sumsq(o) * (1.0 / B_V) + EPS) * g * (1.0 - lam_init)
        o_ref[:, n_a + h * B_V:n_a + (h + 1) * B_V] = y.astype(BF16)


def _merge_out_kernel(oa_ref, ob_ref, g_ref, lv_ref, w_ref, res_ref, gate_ref, o_ref, a_scr, *, lam_init):
    @pl.when(pl.program_id(1) == 0)
    def _():
        _merge_ab_kernel(oa_ref, ob_ref, g_ref, lv_ref, a_scr, lam_init=lam_init)

    o_ref[...] = res_ref[...] + gate_ref[0] * jnp.dot(a_scr[...], w_ref[...], preferred_element_type=F32)


def _merge_out(oa, ob, g_sub, lam_vec, lam_init, w, res, gate, *, name):
    n = oa.shape[0]
    k, m = w.shape
    wa, wb = oa.shape[1], ob.shape[1]
    per_mod = n // gate.shape[0]
    tm = _pick(per_mod, (1024, 512, 256, 128))
    tn = _pick(m, (512, 256, 128))
    tiles_per_mod = per_mod // tm
    return pl.pallas_call(
        functools.partial(_merge_out_kernel, lam_init=lam_init),
        grid=(n // tm, m // tn),
        in_specs=[
            pl.BlockSpec((tm, wa), lambda i, j: (i, 0)),
            pl.BlockSpec((tm, wb), lambda i, j: (i, 0)),
            pl.BlockSpec((1, B_V), lambda i, j: (0, 0)),
            pl.BlockSpec((4, B_DIM), lambda i, j: (0, 0)),
            pl.BlockSpec((None, k, tn), lambda i, j: (j, 0, 0)),
            pl.BlockSpec((tm, tn), lambda i, j: (i, j)),
            pl.BlockSpec((1, 1, tn), lambda i, j: (i // tiles_per_mod, 0, j)),
        ],
        out_specs=pl.BlockSpec((tm, tn), lambda i, j: (i, j)),
        out_shape=jax.ShapeDtypeStruct((n, m), F32),
        scratch_shapes=[pltpu.VMEM((tm, k), BF16)],
        compiler_params=_params("parallel", "arbitrary"),
        name=name,
    )(oa, ob, g_sub.reshape(1, B_V), lam_vec, _col_tiles(w, tn), res, gate)


def _top_rows(s, k):
    vals = []
    cur = s
    for _ in range(k):
        mx = jnp.max(cur, axis=0, keepdims=True)
        vals.append(mx)
        cur = jnp.where(cur == mx, NEG_INF, cur)
    return jnp.concatenate(vals, axis=0)


def _sort_desc(a, bitonic):
    n = len(a)
    a = list(a)
    for k in ((n,) if bitonic else [2 << e for e in range(n.bit_length() - 1)]):
        j = k // 2
        while j >= 1:
            for i in range(n):
                l = i ^ j
                if l > i:
                    hi, lo = jnp.maximum(a[i], a[l]), jnp.minimum(a[i], a[l])
                    a[i], a[l] = (hi, lo) if (i & k) == 0 or bitonic else (lo, hi)
            j //= 2
    return a


def _top16_of_128(s):
    n = P_TOPK
    a = _sort_desc([s[8 * i:8 * i + 8] for i in range(n)], False)
    for shift in (4, 2, 1):
        a = _sort_desc([jnp.maximum(a[i], pltpu.roll(a[n - 1 - i], shift, 0)) for i in range(n)], True)
    return jnp.concatenate([x[0:1] for x in a], axis=0)


def _peer_stats_kernel(q_ref, keys_ref, c1_ref, e2_ref, th_ref):
    dn = (((1,), (1,)), ((), ()))
    s1 = lax.dot_general(keys_ref[0], q_ref[:, 0:N_KEYS], dn, preferred_element_type=F32)
    s2 = lax.dot_general(keys_ref[1], q_ref[:, N_KEYS:2 * N_KEYS], dn, preferred_element_type=F32)
    v1 = _top16_of_128(s1)
    v2 = _top16_of_128(s2)
    n_b = [P_TOPK // (a + 1) for a in range(P_TOPK)]
    cand = jnp.concatenate([v1[a:a + 1] + v2[0:n_b[a]] for a in range(P_TOPK)], axis=0)
    best = _top_rows(cand, P_TOPK)
    inv_z = 1.0 / jnp.sum(jnp.exp(best - best[0:1]), axis=0, keepdims=True)
    c1_ref[0] = jnp.exp(s1 - v1[0:1]) * inv_z
    e2_ref[0] = jnp.exp(s2 - v2[0:1])
    c1v = jnp.exp(v1 - v1[0:1]) * inv_z
    e2v = jnp.exp(v2 - v2[0:1])
    gates = jnp.concatenate([c1v[a:a + 1] * e2v[0:n_b[a]] for a in range(P_TOPK)], axis=0)
    th_ref[0] = jnp.min(jnp.where(cand >= best[P_TOPK - 1:P_TOPK], gates, jnp.inf), axis=0, keepdims=True)


def _peer_stats(q, keys):
    n = q.shape[0]
    tm = _pick(n, (256, 128))
    big = pl.BlockSpec((1, N_KEYS, tm), lambda i, h: (h, 0, i))
    big_shape = jax.ShapeDtypeStruct((P_HEADS, N_KEYS, n), F32)
    return pl.pallas_call(
        _peer_stats_kernel,
        grid=(n // tm, P_HEADS),
        in_specs=[
            pl.BlockSpec((tm, P_DQ), lambda i, h: (i, h)),
            pl.BlockSpec((2, N_KEYS, P_DQ // 2), lambda i, h: (h, 0, 0)),
        ],
        out_specs=[big, big, pl.BlockSpec((1, 1, tm), lambda i, h: (h, 0, i))],
        out_shape=[big_shape, big_shape, jax.ShapeDtypeStruct((P_HEADS, 1, n), F32)],
        compiler_params=_params("parallel", "parallel"),
        name="peer_stats",
    )(q, keys)


_GELU_C1 = -2.0 * math.sqrt(2.0 / math.pi) * LOG2_E
_GELU_C3 = 0.044715 * _GELU_C1


def _gated_gelu_tanh(m, a):
    e = jnp.exp2(a * (_GELU_C3 * (a * a) + _GELU_C1))
    return (m * a) / (1.0 + e)


def _peer_dense_kernel(h_ref, u_ref, vt_ref, c1_ref, e2_ref, th_ref, res_ref, gate_ref,
                       o_ref, acc_ref, at_ref, wt_ref, *, te, tm, tc):
    j = pl.program_id(1)

    @pl.when(j == 0)
    def _():
        acc_ref[...] = jnp.zeros(acc_ref.shape, F32)

    at_ref[...] = jnp.dot(u_ref[...], h_ref[...], preferred_element_type=F32)
    for i1 in range(te // N_KEYS):
        rows = slice(i1 * N_KEYS, (i1 + 1) * N_KEYS)
        for c in range(tm // tc):
            cols = slice(c * tc, (c + 1) * tc)
            m = jnp.zeros((N_KEYS, tc), F32)
            for h in range(P_HEADS):
                g = e2_ref[h, :, cols] * c1_ref[h, i1:i1 + 1, cols]
                m = m + jnp.where(g >= th_ref[h, :, cols], g, 0.0)
            wt_ref[rows, cols] = _gated_gelu_tanh(m, at_ref[rows, cols]).astype(BF16)
    acc_ref[...] += jnp.dot(vt_ref[...], wt_ref[...], preferred_element_type=F32)

    @pl.when(j == pl.num_programs(1) - 1)
    def _():
        o_ref[...] = res_ref[...] + gate_ref[0] * acc_ref[...].T


def _peer_dense(h, u, vt, layer, stats, res, gate):
    d, n = h.shape
    e = u.shape[1]
    c1, e2, theta = stats
    per_mod = n // gate.shape[0]
    tm = _pick(per_mod, (512, 256, 128))
    te = PEER_TE
    tc = _pick(tm, (256, 128))
    tiles_per_mod = per_mod // tm
    once = dict(pipeline_mode=pl.Buffered(1))
    big = pl.BlockSpec((P_HEADS, N_KEYS, tm), lambda i, j: (0, 0, i), **once)
    row = pl.BlockSpec((P_HEADS, te // N_KEYS, tm), lambda i, j: (0, j, i))
    return pl.pallas_call(
        functools.partial(_peer_dense_kernel, te=te, tm=tm, tc=tc),
        grid=(n // tm, e // te),
        in_specs=[
            pl.BlockSpec((d, tm), lambda i, j: (0, i)),
            pl.BlockSpec((None, te, d), lambda i, j: (layer, j, 0)),
            pl.BlockSpec((None, None, d, te), lambda i, j: (layer, j, 0, 0)),
            row, big,
            pl.BlockSpec((P_HEADS, 1, tm), lambda i, j: (0, 0, i)),
            pl.BlockSpec((tm, d), lambda i, j: (i, 0), **once),
            pl.BlockSpec((1, 1, d), lambda i, j: (i // tiles_per_mod, 0, 0)),
        ],
        out_specs=pl.BlockSpec((tm, d), lambda i, j: (i, 0), **once),
        out_shape=jax.ShapeDtypeStruct((n, d), F32),
        scratch_shapes=[pltpu.VMEM((d, tm), F32), pltpu.VMEM((te, tm), F32), pltpu.VMEM((te, tm), BF16)],
        compiler_params=_params("parallel", "arbitrary"),
        name="peer_dense",
    )(h, u, vt, c1, e2, theta, res, gate)


def _rope_tables(seq, rot_dim):
    n_freq = rot_dim // 4
    freqs = ROPE_BASE ** (-jnp.arange(n_freq, dtype=F32) / n_freq)
    pos = jnp.arange(seq)
    row = (pos // GRID_W).astype(F32)[:, None] * freqs
    col = (pos % GRID_W).astype(F32)[:, None] * freqs
    zero = jnp.zeros_like(row)
    cos = jnp.concatenate([jnp.cos(row), jnp.cos(row), jnp.cos(col), jnp.cos(col)], axis=-1)
    s_up = jnp.concatenate([-jnp.sin(row), zero, -jnp.sin(col), zero], axis=-1)
    s_dn = jnp.concatenate([zero, jnp.sin(row), zero, jnp.sin(col)], axis=-1)
    return cos, s_up, s_dn


def _tile_tables(tabs, reps, width):
    out = []
    for idx, t in enumerate(tabs):
        t = jnp.tile(t, (1, reps))
        pad = width - t.shape[1]
        if pad:
            fill = jnp.ones if idx == 0 else jnp.zeros
            t = jnp.concatenate([t, fill((t.shape[0], pad), F32)], axis=-1)
        out.append(t)
    return out


def _pad_lanes(v, width):
    return jnp.pad(v, (0, width - v.shape[0])).reshape(1, width)


def _layout_w_in_ab(w):
    d = w.shape[0]
    cq, ckv, kr, bq, bk, bv = jnp.split(w, [768, 1280, 1344, 2368, 3392], axis=1)
    return jnp.concatenate([cq, kr, jnp.zeros((d, AB_BQ - AB_KR - A_ROPE), w.dtype), bq, bk, bv, ckv],
                           axis=1).astype(BF16)


def _layout_w_uq(w):
    k = w.shape[0]
    w3 = w.reshape(k, A_HEADS, A_QK)
    nope = w3[:, :, :A_NOPE].reshape(k, A_HEADS * A_NOPE)
    rope = jnp.pad(w3[:, :, A_NOPE:], ((0, 0), (0, 0), (0, LANE_V7X - A_ROPE))).reshape(k, A_HEADS * LANE_V7X)
    return jnp.concatenate([nope, rope], axis=1).astype(BF16)


def _mixer_ab(tok, mod, p, tables, *, batch, seq, rope):
    proj = _nmm(tok, p["g_norm1"], p["w_in"], shift=mod[0], scale=mod[1], name="ab_in")
    qa_raw = _nmm(proj, p["g_cq"], p["w_uq"], k_block=AB_CQ // A_Q_LORA, name="ab_uq")
    kva_raw = _nmm(proj, p["g_ckv"], p["w_ukv"], k_block=AB_CKV // A_KV_LORA, name="ab_ukv")
    sa, sb = A_QK ** -0.5 * LOG2_E, B_DIM ** -0.5 * LOG2_E
    (qa,) = _prep_call(
        functools.partial(_prep_aq_kernel, rope=rope, scale=sa), [qa_raw], [(qa_raw.shape[1], 0)],
        [p["gqa_n"], p["gqa_r"]], tables["a"] if rope else None, [((A_HEADS, 1), 2 * LANE_V7X)],
        batch=batch, seq=seq, name="prep_aq")
    ka, va = _prep_call(
        functools.partial(_prep_akv_kernel, rope=rope), [kva_raw, proj],
        [(kva_raw.shape[1], 0), (LANE_V7X, AB_KR // LANE_V7X)],
        [p["gka_n"], p["gka_r"]], tables["a"] if rope else None,
        [((A_HEADS,), 2 * LANE_V7X), ((A_HEADS,), A_V)], batch=batch, seq=seq, name="prep_akv")
    wb = B_HEADS * 2 * B_DIM
    qb, kb, vb = _prep_call(
        functools.partial(_prep_b_kernel, rope=rope, scale=sb), [proj, proj, proj],
        [(wb, AB_BQ // wb), (wb, AB_BK // wb), (wb, AB_BV // wb)],
        [p["gqb"], p["gkb"]], tables["b"] if rope else None,
        [((2 * B_HEADS, 1), B_DIM), ((2 * B_HEADS,), B_DIM), ((B_HEADS,), B_V)],
        batch=batch, seq=seq, name="prep_b")
    return (qa, ka, va), (qb, kb, vb)


def _mixer_c(tok, mod, p, tables, *, batch, seq, rope):
    proj = _nmm(tok, p["g_norm1"], p["w_in"], shift=mod[0], scale=mod[1], name="c_in")
    wq = C_HEADS * C_DIM
    wkv = 2 * C_KV_HEADS * C_DIM
    return _prep_call(
        functools.partial(_prep_c_kernel, rope=rope, scale=C_DIM ** -0.5 * LOG2_E), [proj, proj],
        [(wq, 0), (wkv, wq // wkv)], [p["gq"], p["gk"]], tables["c"] if rope else None,
        [((C_KV_HEADS, C_GROUP), C_DIM), ((C_KV_HEADS,), C_DIM), ((C_KV_HEADS,), C_DIM)],
        batch=batch, seq=seq, name="prep_c")


def _peer(tok, mod, gate, p, name):
    q, h = _nmm(tok, p["g_norm2"], p["w_pq"], shift=mod[0], scale=mod[1], emit_h=True, out_dtype=BF16, name=name)
    stats = _peer_stats(q, p["keys"])
    return _peer_dense(h, p["u_all"], p["vt_all"], p["layer"], stats, tok, gate)


def kernel(x, c, ctx, c_ctx, w_mod, b_mod, g_norm1, g_norm2, w_in_ab, g_cq, w_uq, g_ckv, w_ukv, g_qn_a, g_kn_a,
           lam_vec, g_qn_b, g_kn_b, g_sub_b, w_out_ab, w_in_c, g_qn_c, g_kn_c, w_out_c, w_pq, sub_keys,
           expert_u, expert_v):
    batch, seq, d = x.shape
    ctx_len = ctx.shape[1]
    depth = w_mod.shape[0]
    assert batch + 1 <= 8

    cz = jnp.concatenate([c, c_ctx[None, :], jnp.zeros((8 - batch - 1, d), F32)], axis=0)
    mod_all = _modulation(cz, w_mod, b_mod).reshape(depth, 8, 6, d)

    tab64 = _rope_tables(seq, A_ROPE)
    tables = {
        "a": _tile_tables(tab64, 1, LANE_V7X),
        "b": _tile_tables(tab64, 2, LANE_V7X),
        "c": _rope_tables(seq, C_DIM),
    }

    xt = x.reshape(batch * seq, d)
    zt = ctx.reshape(batch * ctx_len, d)
    u_all = expert_u.astype(BF16)
    n_exp = expert_v.shape[1]
    vt_all = jnp.swapaxes(expert_v.reshape(depth, n_exp // PEER_TE, PEER_TE, d), 2, 3).astype(BF16)

    for layer in range(depth):
        last = layer == depth - 1
        e = layer // 2
        mx = [mod_all[layer, :batch, i][:, None, :] for i in range(6)]
        mz = [mod_all[layer, batch:batch + 1, i][:, None, :] for i in range(6)]
        pp = {
            "g_norm2": g_norm2[layer], "w_pq": w_pq[layer].astype(BF16),
            "keys": sub_keys[layer].reshape(2 * P_HEADS, N_KEYS, P_DQ // 2).astype(BF16),
            "u_all": u_all, "vt_all": vt_all, "layer": layer,
        }
        if layer % 2 == 0:
            lam_init = 0.8 - 0.6 * math.exp(-0.3 * layer)
            p = {
                "g_norm1": g_norm1[layer], "w_in": _layout_w_in_ab(w_in_ab[e]),
                "g_cq": g_cq[e], "w_uq": _layout_w_uq(w_uq[e]),
                "g_ckv": g_ckv[e], "w_ukv": w_ukv[e].astype(BF16),
                "gqa_n": g_qn_a[e][:A_NOPE].reshape(1, A_NOPE), "gqa_r": _pad_lanes(g_qn_a[e][A_NOPE:], LANE_V7X),
                "gka_n": g_kn_a[e][:A_NOPE].reshape(1, A_NOPE), "gka_r": _pad_lanes(g_kn_a[e][A_NOPE:], LANE_V7X),
                "gqb": jnp.tile(g_qn_b[e], 2).reshape(1, 2 * B_DIM), "gkb": jnp.tile(g_kn_b[e], 2).reshape(1, 2 * B_DIM),
            }
            w_out = w_out_ab[e].astype(BF16)
            (qa, ka, va), (qb, kb, vb) = _mixer_ab(xt, mx, p, tables, batch=batch, seq=seq, rope=True)
            (qaz, kaz, vaz), (qbz, kbz, vbz) = _mixer_ab(zt, mz, p, tables, batch=batch, seq=ctx_len, rope=False)
            oa = _flash(qa, [(ka, va), (kaz, vaz)], out_dtype=BF16, name="flash_a")
            ob = _flash(qb, [(kb, vb), (kbz, vbz)], out_dtype=F32, name="flash_b")
            merge = functools.partial(_merge_out, g_sub=g_sub_b[e], lam_vec=lam_vec[e], lam_init=lam_init, w=w_out)
            out_x = functools.partial(merge, oa.reshape(batch * seq, -1), ob.reshape(batch * seq, -1))
            if not last:
                oaz = _flash(qaz, [(kaz, vaz)], out_dtype=BF16, name="flash_az")
                obz = _flash(qbz, [(kbz, vbz)], out_dtype=F32, name="flash_bz")
                out_z = functools.partial(merge, oaz.reshape(batch * ctx_len, -1), obz.reshape(batch * ctx_len, -1))
        else:
            p = {
                "g_norm1": g_norm1[layer], "w_in": w_in_c[e].astype(BF16),
                "gq": g_qn_c[e].reshape(1, C_DIM), "gk": g_kn_c[e].reshape(1, C_DIM),
            }
            w_out = w_out_c[e].astype(BF16)
            q, k, v = _mixer_c(xt, mx, p, tables, batch=batch, seq=seq, rope=True)
            qz, kz, vz = _mixer_c(zt, mz, p, tables, batch=batch, seq=ctx_len, rope=False)
            att = _flash(q, [(k, v), (kz, vz)], out_dtype=BF16, name="flash_c").reshape(batch * seq, -1)
            out_x = functools.partial(_mm_res, att, w_out)
            if not last:
                attz = _flash(qz, [(kz, vz)], out_dtype=BF16, name="flash_cz").reshape(batch * ctx_len, -1)
                out_z = functools.partial(_mm_res, attz, w_out)

        xt = out_x(res=xt, gate=mx[2], name="out_proj")
        xt = _peer(xt, (mx[3], mx[4]), mx[5], pp, "peer_q")
        if not last:
            zt = out_z(res=zt, gate=mz[2], name="out_proj_z")
            zt = _peer(zt, (mz[3], mz[4]), mz[5], pp, "peer_qz")
    return xt.reshape(batch, seq, d)
```
